```python
import jax, jax.numpy as jnp
from jax import lax
import numpy as np

D_MODEL = 1024
BATCH = 4
SEQ = 8192
DEPTH = 2

MIX_WIDTH = D_MODEL
GM_WIDTH = MIX_WIDTH // 4
GM_HEADS = 4
GM_HEAD_DIM = GM_WIDTH // GM_HEADS
GM_CHUNK = 128
RET_WIDTH = MIX_WIDTH // 2
RET_HEADS = 4
RET_HEAD_DIM = RET_WIDTH // RET_HEADS
RET_CHUNK = 128
CONV_WIDTH = MIX_WIDTH - GM_WIDTH - RET_WIDTH
CONV_KERNEL = 31
IN_WIDTH = 2 * GM_WIDTH + 4 * RET_WIDTH + 2 * CONV_WIDTH
FFN_HIDDEN = ((8 * D_MODEL // 3 + 255) // 256) * 256
ROPE_BASE = 10000.0
EPS = 1e-6

kernel_name = 'hybrid_gmlp_retention_conformer_encoder'


def _rmsnorm(x, g):
    x32 = x.astype(jnp.float32)
    y = x32 * lax.rsqrt(jnp.mean(x32 * x32, axis=-1, keepdims=True) + EPS)
    return (y * g.astype(jnp.float32)).astype(x.dtype)


def _standardize(x):
    x32 = x.astype(jnp.float32)
    mu = jnp.mean(x32, axis=-1, keepdims=True)
    var = jnp.mean(jnp.square(x32 - mu), axis=-1, keepdims=True)
    return (x32 - mu) * lax.rsqrt(var + EPS)


def _layernorm(x, g, b):
    return (_standardize(x) * g.astype(jnp.float32) + b.astype(jnp.float32)).astype(x.dtype)


def _spatial_gating(u, v, ln_g, ln_b, w_s, b_s):
    bsz, seq, _ = v.shape
    n = seq // GM_CHUNK
    v = _layernorm(v, ln_g, ln_b).reshape(bsz, n, GM_CHUNK, GM_HEADS, GM_HEAD_DIM)
    mixed = jnp.einsum('hpq,bnqhd->bnphd', w_s, v) + b_s.T[:, :, None]
    return (u.reshape(mixed.shape) * mixed).reshape(bsz, seq, GM_WIDTH)


def _rotary(t, cos, sin):
    t1, t2 = jnp.split(t, 2, axis=-1)
    return jnp.concatenate([t1 * cos - t2 * sin, t1 * sin + t2 * cos], axis=-1)


def _retention_direction(q, k, v, gamma, include_diag):
    dt = q.dtype
    idx = jnp.arange(RET_CHUNK, dtype=jnp.float32)
    log_g = jnp.log(gamma)[:, None]
    diff = idx[:, None] - idx[None, :]
    mask = (diff >= 0) if include_diag else (diff > 0)
    d_intra = jnp.where(mask, jnp.exp(log_g[:, :, None] * jnp.where(mask, diff, 0.0)), 0.0).astype(dt)
    zeta = jnp.exp(log_g * (RET_CHUNK - 1 - idx)).astype(dt)
    xi = jnp.exp(log_g * (idx + 1)).astype(dt)
    gamma_c = jnp.exp(log_g * RET_CHUNK).astype(dt)[:, :, None]
    scores = jnp.einsum('bhncd,bhnmd->bhncm', q, k) * d_intra[:, None]
    intra = jnp.einsum('bhncm,bhnme->bhnce', scores, v)
    kv = jnp.einsum('bhnmd,bhnme,hm->nbhde', k, v, zeta)

    def step(state, kv_n):
        return gamma_c * state + kv_n, state

    _, prev = lax.scan(step, jnp.zeros_like(kv[0]), kv)
    cross = jnp.einsum('bhncd,nbhde,hc->bhnce', q, prev, xi)
    return intra + cross


def _retention(q, k, v, g, cos, sin):
    bsz, seq, _ = q.shape
    n = seq // RET_CHUNK
    shp = (bsz, seq, RET_HEADS, RET_HEAD_DIM)
    q = _rotary(q.reshape(shp), cos, sin)
    k = _rotary(k.reshape(shp), cos, sin) * (RET_HEAD_DIM ** -0.5)
    v = v.reshape(shp)

    def chunk(t):
        return t.reshape(bsz, n, RET_CHUNK, RET_HEADS, RET_HEAD_DIM).transpose(0, 3, 1, 2, 4)

    def unchunk(t):
        return t.transpose(0, 2, 3, 1, 4).reshape(shp)

    def rev(t):
        return jnp.flip(t, axis=1)

    gamma_fwd = 1.0 - jnp.exp2(-5.0 - jnp.arange(RET_HEADS, dtype=jnp.float32))
    gamma_bwd = gamma_fwd[::-1]
    fwd = unchunk(_retention_direction(chunk(q), chunk(k), chunk(v), gamma_fwd, True))
    bwd = rev(unchunk(_retention_direction(chunk(rev(q)), chunk(rev(k)), chunk(rev(v)), gamma_bwd, False)))
    o = _standardize(fwd + bwd).astype(q.dtype)
    return o.reshape(bsz, seq, RET_WIDTH) * jax.nn.silu(g)


def _conformer_conv(a, gate, dw_w, dw_b, ln_g, ln_b):
    h = a * jax.nn.sigmoid(gate)
    pad = CONV_KERNEL // 2
    h = lax.conv_general_dilated(h, dw_w[:, None, :], window_strides=(1,), padding=[(pad, pad)],
                                 dimension_numbers=('NWC', 'WIO', 'NWC'),
                                 feature_group_count=CONV_WIDTH) + dw_b
    return jax.nn.silu(_layernorm(h, ln_g, ln_b))


def setup_inputs(seed: int = 0) -> dict:
    key = jax.random.key(seed)
    ks = jax.random.split(key, 16)
    f32 = jnp.float32

    def nrm(k, shape, scale):
        return jax.random.normal(k, shape, f32) * scale

    return {
        'x': nrm(ks[0], (BATCH, SEQ, D_MODEL), 1.0),
        'norm1_g': 1.0 + nrm(ks[1], (DEPTH, D_MODEL), 0.02),
        'w_in': nrm(ks[2], (DEPTH, D_MODEL, IN_WIDTH), D_MODEL ** -0.5),
        'gm_ln_g': 1.0 + nrm(ks[3], (DEPTH, GM_WIDTH), 0.02),
        'gm_ln_b': nrm(ks[4], (DEPTH, GM_WIDTH), 0.02),
        'gm_ws': nrm(ks[5], (DEPTH, GM_HEADS, GM_CHUNK, GM_CHUNK), GM_CHUNK ** -0.5),
        'gm_bs': 1.0 + nrm(ks[6], (DEPTH, GM_HEADS, GM_CHUNK), 0.02),
        'conv_w': nrm(ks[7], (DEPTH, CONV_KERNEL, CONV_WIDTH), CONV_KERNEL ** -0.5),
        'conv_b': nrm(ks[8], (DEPTH, CONV_WIDTH), 0.02),
        'conv_ln_g': 1.0 + nrm(ks[9], (DEPTH, CONV_WIDTH), 0.02),
        'conv_ln_b': nrm(ks[10], (DEPTH, CONV_WIDTH), 0.02),
        'w_out': nrm(ks[11], (DEPTH, MIX_WIDTH, D_MODEL), MIX_WIDTH ** -0.5),
        'norm2_g': 1.0 + nrm(ks[12], (DEPTH, D_MODEL), 0.02),
        'w_ffn_in': nrm(ks[13], (DEPTH, D_MODEL, 2 * FFN_HIDDEN), D_MODEL ** -0.5),
        'w_ffn_out': nrm(ks[14], (DEPTH, FFN_HIDDEN, D_MODEL), FFN_HIDDEN ** -0.5),
        'final_g': 1.0 + nrm(ks[15], (D_MODEL,), 0.02),
    }


def reference(x, norm1_g, w_in, gm_ln_g, gm_ln_b, gm_ws, gm_bs, conv_w, conv_b, conv_ln_g,
              conv_ln_b, w_out, norm2_g, w_ffn_in, w_ffn_out, final_g):
    seq = x.shape[1]
    pos = jnp.arange(seq, dtype=jnp.float32)
    half = RET_HEAD_DIM // 2
    inv_freq = ROPE_BASE ** (-jnp.arange(half, dtype=jnp.float32) / half)
    ang = pos[:, None] * inv_freq[None, :]
    cos = jnp.cos(ang)[:, None, :].astype(x.dtype)
    sin = jnp.sin(ang)[:, None, :].astype(x.dtype)
    splits = np.cumsum([GM_WIDTH, GM_WIDTH, RET_WIDTH, RET_WIDTH, RET_WIDTH, RET_WIDTH, CONV_WIDTH])
    for l in range(DEPTH):
        h = _rmsnorm(x, norm1_g[l])
        proj = h @ w_in[l]
        gm_u, gm_v, q, k, v, g, cv_a, cv_gate = jnp.split(proj, splits, axis=-1)
        y_gm = _spatial_gating(jax.nn.gelu(gm_u, approximate=False), jax.nn.gelu(gm_v, approximate=False),
                               gm_ln_g[l], gm_ln_b[l], gm_ws[l], gm_bs[l])
        y_ret = _retention(q, k, v, g, cos, sin)
        y_cv = _conformer_conv(cv_a, cv_gate, conv_w[l], conv_b[l], conv_ln_g[l], conv_ln_b[l])
        x = x + jnp.concatenate([y_gm, y_ret, y_cv], axis=-1) @ w_out[l]
        h = _rmsnorm(x, norm2_g[l])
        gate, up = jnp.split(h @ w_ffn_in[l], 2, axis=-1)
        x = x + (jax.nn.silu(gate) * up) @ w_ffn_out[l]
    return _rmsnorm(x, final_g)
```

```python
import functools

import numpy as np
import jax
import jax.numpy as jnp
from jax import lax
from jax.experimental import pallas as pl
from jax.experimental.pallas import tpu as pltpu

F32 = jnp.float32
BF16 = jnp.bfloat16

D_MODEL = 1024
GM_WIDTH = 256
GM_HEADS = 4
GM_HEAD_DIM = GM_WIDTH // GM_HEADS
CHUNK = 128
RET_WIDTH = 512
RET_HEADS = 4
RET_HEAD_DIM = RET_WIDTH // RET_HEADS
CONV_WIDTH = 256
CONV_KERNEL = 31
CONV_PAD = CONV_KERNEL // 2
IN_WIDTH = 2 * GM_WIDTH + 4 * RET_WIDTH + 2 * CONV_WIDTH
FFN_HIDDEN = 2816
ROPE_BASE = 10000.0
EPS = 1e-6

COL_GM = 0
COL_Q = 2 * GM_WIDTH
COL_V = COL_Q + 2 * RET_WIDTH
COL_CONV = COL_V + 2 * RET_WIDTH

TOKEN_TILE = 512
HALO_ROWS = 16
STATE_CHUNKS = 8
CONV_ROW_BLOCK = 64
FFN_COL_BLOCK = 256
VMEM_LIMIT_BYTES = 56 * 1024 * 1024


def _gelu(x):
    return 0.5 * x * (1.0 + lax.erf(x * np.float32(1.0 / np.sqrt(2.0))))


def _sigmoid(x):
    return 1.0 / (1.0 + jnp.exp(-x))


def _standardize(x):
    mu = jnp.mean(x, axis=-1, keepdims=True)
    xc = x - mu
    var = jnp.mean(xc * xc, axis=-1, keepdims=True)
    return xc * lax.rsqrt(var + EPS)


def _rmsnorm(x, g):
    return x * lax.rsqrt(jnp.mean(x * x, axis=-1, keepdims=True) + EPS) * g


def _retention_tables():
    idx = np.arange(CHUNK, dtype=np.float32)
    gamma_f = (1.0 - np.exp2(-5.0 - np.arange(RET_HEADS, dtype=np.float32))).astype(np.float32)
    gamma_b = gamma_f[::-1]
    lf = np.log(gamma_f)[:, None]
    lb = np.log(gamma_b)[:, None]
    diff = idx[:, None] - idx[None, :]
    dmask = np.where(diff >= 0,
                     np.exp(lf[:, :, None] * np.maximum(diff, 0.0)),
                     np.exp(lb[:, :, None] * np.maximum(-diff, 0.0))).astype(np.float32)
    zeta_f = np.exp(lf * (CHUNK - 1 - idx))
    zeta_b = np.exp(lb * idx)
    xi_f = np.exp(lf * (idx + 1))
    xi_b = np.exp(lb * (CHUNK - idx))

    def per_row(t):
        return np.repeat(t.T.astype(np.float32), RET_HEAD_DIM, axis=1)

    gc_f = [float(np.exp(np.float32(l) * np.float32(CHUNK))) for l in lf[:, 0]]
    gc_b = [float(np.exp(np.float32(l) * np.float32(CHUNK))) for l in lb[:, 0]]
    return dmask, per_row(zeta_f), per_row(zeta_b), per_row(xi_f), per_row(xi_b), gc_f, gc_b


def _rotary_tables(seq):
    half = RET_HEAD_DIM // 2
    inv_freq = ROPE_BASE ** (-np.arange(half, dtype=np.float64) / half)
    ang = np.arange(seq, dtype=np.float64)[:, None] * inv_freq[None, :]
    cos, sin = np.cos(ang), np.sin(ang)
    cos2 = np.concatenate([cos, cos], axis=-1).astype(np.float32)
    sin2 = np.concatenate([-sin, sin], axis=-1).astype(np.float32)
    return cos2, sin2


def _in_proj_body(x_ref, g1_ref, w_ref, cos_ref, sin_ref, lng_ref, lnb_ref, ws_ref, bs_ref,
                  ygm_ref, q_ref, k_ref, v_ref, sg_ref, hglu_ref):
    h = _rmsnorm(x_ref[...], g1_ref[...]).astype(BF16)

    def proj(lo, hi):
        return jnp.dot(h, w_ref[:, lo:hi], preferred_element_type=F32)

    uv = proj(COL_GM, COL_Q)
    u = _gelu(uv[:, :GM_WIDTH])
    v = _gelu(uv[:, GM_WIDTH:])
    vln = (_standardize(v) * lng_ref[...] + lnb_ref[...]).astype(BF16)
    lane_head = lax.broadcasted_iota(jnp.int32, (CHUNK, GM_WIDTH), 1) // GM_HEAD_DIM
    for c in range(TOKEN_TILE // CHUNK):
        rows = slice(c * CHUNK, (c + 1) * CHUNK)
        full = jnp.dot(ws_ref[...], vln[rows], preferred_element_type=F32)
        mixed = full[0:CHUNK]
        for hd in range(1, GM_HEADS):
            mixed = jnp.where(lane_head == hd, full[hd * CHUNK:(hd + 1) * CHUNK], mixed)
        ygm_ref[rows, :] = (u[rows] * (mixed + bs_ref[...])).astype(BF16)

    cos2 = cos_ref[...]
    sin2 = sin_ref[...]
    qk = proj(COL_Q, COL_V)
    scale = np.float32(RET_HEAD_DIM ** -0.5)
    for hd in range(RET_HEADS):
        cols = slice(hd * RET_HEAD_DIM, (hd + 1) * RET_HEAD_DIM)
        t = qk[:, cols]
        q_ref[:, cols] = (t * cos2 + pltpu.roll(t, RET_HEAD_DIM // 2, 1) * sin2).astype(BF16)
        t = qk[:, RET_WIDTH + hd * RET_HEAD_DIM:RET_WIDTH + (hd + 1) * RET_HEAD_DIM]
        k_ref[:, cols] = ((t * cos2 + pltpu.roll(t, RET_HEAD_DIM // 2, 1) * sin2) * scale).astype(BF16)
    vg = proj(COL_V, COL_CONV)
    v_ref[...] = vg[:, :RET_WIDTH].astype(BF16)
    g = vg[:, RET_WIDTH:]
    sg_ref[...] = (g * _sigmoid(g)).astype(BF16)

    ag = proj(COL_CONV, IN_WIDTH)
    hglu_ref[...] = ag[:, :CONV_WIDTH] * _sigmoid(ag[:, CONV_WIDTH:])


def _in_proj(x2, g1, w_in, cos2, sin2, lng, lnb, ws, bs, seq):
    m = x2.shape[0]
    tiles_per_seq = seq // TOKEN_TILE
    tok = lambda width: pl.BlockSpec((TOKEN_TILE, width), lambda t: (t, 0))
    const = lambda shape: pl.BlockSpec(shape, lambda t: (0,) * len(shape))
    return pl.pallas_call(
        _in_proj_body,
        grid=(m // TOKEN_TILE,),
        in_specs=[
            tok(D_MODEL), const((1, D_MODEL)), const((D_MODEL, IN_WIDTH)),
            pl.BlockSpec((TOKEN_TILE, RET_HEAD_DIM), lambda t: (t % tiles_per_seq, 0)),
            pl.BlockSpec((TOKEN_TILE, RET_HEAD_DIM), lambda t: (t % tiles_per_seq, 0)),
            const((1, GM_WIDTH)), const((1, GM_WIDTH)),
            const((GM_HEADS * CHUNK, CHUNK)), const((CHUNK, GM_WIDTH)),
        ],
        out_specs=[tok(GM_WIDTH), tok(RET_WIDTH), tok(RET_WIDTH), tok(RET_WIDTH), tok(RET_WIDTH),
                   tok(CONV_WIDTH)],
        out_shape=[
            jax.ShapeDtypeStruct((m, GM_WIDTH), BF16),
            jax.ShapeDtypeStruct((m, RET_WIDTH), BF16),
            jax.ShapeDtypeStruct((m, RET_WIDTH), BF16),
            jax.ShapeDtypeStruct((m, RET_WIDTH), BF16),
            jax.ShapeDtypeStruct((m, RET_WIDTH), BF16),
            jax.ShapeDtypeStruct((m, CONV_WIDTH), F32),
        ],
        compiler_params=pltpu.CompilerParams(
            dimension_semantics=("parallel",), vmem_limit_bytes=VMEM_LIMIT_BYTES),
        name="in_proj",
    )(x2, g1, w_in, cos2, sin2, lng, lnb, ws, bs)


def _ret_state_body(gc_f, gc_b, kf_ref, vf_ref, kb_ref, vb_ref, zf_ref, zb_ref,
                    sf_ref, sb_ref, st_ref):
    @pl.when(pl.program_id(1) == 0)
    def _():
        st_ref[...] = jnp.zeros_like(st_ref)

    def chunk_kv(k_ref, v_ref, z_ref, j):
        rows = slice(j * CHUNK, (j + 1) * CHUNK)
        vz = (v_ref[rows, :].astype(F32) * z_ref[...]).astype(BF16)
        out = []
        for hd in range(RET_HEADS):
            cols = slice(hd * RET_HEAD_DIM, (hd + 1) * RET_HEAD_DIM)
            out.append(lax.dot_general(k_ref[rows, cols], vz[:, cols], (((0,), (0,)), ((), ())),
                                       preferred_element_type=F32))
        return out

    for d, (k_ref, v_ref, z_ref, s_ref, gc, order) in enumerate((
            (kf_ref, vf_ref, zf_ref, sf_ref, gc_f, range(STATE_CHUNKS)),
            (kb_ref, vb_ref, zb_ref, sb_ref, gc_b, range(STATE_CHUNKS - 1, -1, -1)))):
        for j in order:
            kv = chunk_kv(k_ref, v_ref, z_ref, j)
            for hd in range(RET_HEADS):
                state = st_ref[d, hd]
                s_ref[j, hd] = state.astype(BF16)
                st_ref[d, hd] = gc[hd] * state + kv[hd]


def _ret_state(k, v, zf, zb, gc_f, gc_b, batch, seq):
    rows = STATE_CHUNKS * CHUNK
    nb = seq // rows
    n_chunks = batch * seq // CHUNK
    fwd = pl.BlockSpec((rows, RET_WIDTH), lambda b, i: (b * nb + i, 0))
    bwd = pl.BlockSpec((rows, RET_WIDTH), lambda b, i: (b * nb + nb - 1 - i, 0))
    const = pl.BlockSpec((CHUNK, RET_WIDTH), lambda b, i: (0, 0))
    state_shape = (STATE_CHUNKS, RET_HEADS, RET_HEAD_DIM, RET_HEAD_DIM)
    return pl.pallas_call(
        functools.partial(_ret_state_body, gc_f, gc_b),
        grid=(batch, nb),
        in_specs=[fwd, fwd, bwd, bwd, const, const],
        out_specs=[
            pl.BlockSpec(state_shape, lambda b, i: (b * nb + i, 0, 0, 0)),
            pl.BlockSpec(state_shape, lambda b, i: (b * nb + nb - 1 - i, 0, 0, 0)),
        ],
        out_shape=[jax.ShapeDtypeStruct((n_chunks, RET_HEADS, RET_HEAD_DIM, RET_HEAD_DIM), BF16)] * 2,
        scratch_shapes=[pltpu.VMEM((2, RET_HEADS, RET_HEAD_DIM, RET_HEAD_DIM), F32)],
        compiler_params=pltpu.CompilerParams(
            dimension_semantics=("parallel", "arbitrary"), vmem_limit_bytes=VMEM_LIMIT_BYTES),
        name="ret_state",
    )(k, v, k, v, zf, zb)


def _mix_out_body(tiles_per_seq, x_ref, ygm_ref, q_ref, k_ref, v_ref, sg_ref,
                  hmain_ref, hprev_ref, hnext_ref, sf_ref, sb_ref, dmask_ref, xif_ref, xib_ref,
                  cw_ref, cb_ref, clg_ref, clb_ref, wout_ref, o_ref, hpad_ref, y_ref):
    t = pl.program_id(0)
    first = (t % tiles_per_seq) == 0
    last = (t % tiles_per_seq) == tiles_per_seq - 1

    y_ref[:, 0:GM_WIDTH] = ygm_ref[...]

    for c in range(TOKEN_TILE // CHUNK):
        rows = slice(c * CHUNK, (c + 1) * CHUNK)
        for hd in range(RET_HEADS):
            cols = slice(hd * RET_HEAD_DIM, (hd + 1) * RET_HEAD_DIM)
            qh = q_ref[rows, cols]
            scores = lax.dot_general(qh, k_ref[rows, cols], (((1,), (1,)), ((), ())),
                                     preferred_element_type=F32)
            p = (scores * dmask_ref[hd]).astype(BF16)
            o = jnp.dot(p, v_ref[rows, cols], preferred_element_type=F32)
            o = o + xif_ref[:, cols] * jnp.dot(qh, sf_ref[c, hd], preferred_element_type=F32)
            o = o + xib_ref[:, cols] * jnp.dot(qh, sb_ref[c, hd], preferred_element_type=F32)
            y_ref[rows, GM_WIDTH + hd * RET_HEAD_DIM:GM_WIDTH + (hd + 1) * RET_HEAD_DIM] = (
                _standardize(o) * sg_ref[rows, cols].astype(F32)).astype(BF16)

    zero_halo = jnp.zeros((HALO_ROWS, CONV_WIDTH), F32)
    hpad_ref[0:HALO_ROWS, :] = jnp.where(first, zero_halo, hprev_ref[...])
    hpad_ref[HALO_ROWS:HALO_ROWS + TOKEN_TILE, :] = hmain_ref[...]
    hpad_ref[HALO_ROWS + TOKEN_TILE:, :] = jnp.where(last, zero_halo, hnext_ref[...])
    for rb in range(TOKEN_TILE // CONV_ROW_BLOCK):
        r0 = rb * CONV_ROW_BLOCK
        acc = jnp.zeros((CONV_ROW_BLOCK, CONV_WIDTH), F32)
        for j in range(CONV_KERNEL):
            start = r0 + HALO_ROWS - CONV_PAD + j
            acc = acc + cw_ref[j:j + 1, :] * hpad_ref[start:start + CONV_ROW_BLOCK, :]
        hc = _standardize(acc + cb_ref[...]) * clg_ref[...] + clb_ref[...]
        y_ref[r0:r0 + CONV_ROW_BLOCK, GM_WIDTH + RET_WIDTH:] = (hc * _sigmoid(hc)).astype(BF16)

    o_ref[...] = x_ref[...] + jnp.dot(y_ref[...], wout_ref[...], preferred_element_type=F32)


def _mix_out(x2, ygm, q, k, v, sg, hglu, sf, sb, dmask, xif, xib, cw, cb, clg, clb, wout, seq):
    m = x2.shape[0]
    tiles_per_seq = seq // TOKEN_TILE
    halo_per_tile = TOKEN_TILE // HALO_ROWS
    n_halo = m // HALO_ROWS
    tok = lambda width: pl.BlockSpec((TOKEN_TILE, width), lambda t: (t, 0))
    const = lambda shape: pl.BlockSpec(shape, lambda t: (0,) * len(shape))
    state = pl.BlockSpec((TOKEN_TILE // CHUNK, RET_HEADS, RET_HEAD_DIM, RET_HEAD_DIM),
                         lambda t: (t, 0, 0, 0))
    return pl.pallas_call(
        functools.partial(_mix_out_body, tiles_per_seq),
        grid=(m // TOKEN_TILE,),
        in_specs=[
            tok(D_MODEL), tok(GM_WIDTH), tok(RET_WIDTH), tok(RET_WIDTH), tok(RET_WIDTH), tok(RET_WIDTH),
            tok(CONV_WIDTH),
            pl.BlockSpec((HALO_ROWS, CONV_WIDTH), lambda t: (jnp.maximum(t * halo_per_tile - 1, 0), 0)),
            pl.BlockSpec((HALO_ROWS, CONV_WIDTH),
                         lambda t: (jnp.minimum((t + 1) * halo_per_tile, n_halo - 1), 0)),
            state, state,
            const((RET_HEADS, CHUNK, CHUNK)), const((CHUNK, RET_WIDTH)), const((CHUNK, RET_WIDTH)),
            const((CONV_KERNEL, CONV_WIDTH)), const((1, CONV_WIDTH)), const((1, CONV_WIDTH)),
            const((1, CONV_WIDTH)), const((D_MODEL, D_MODEL)),
        ],
        out_specs=tok(D_MODEL),
        out_shape=jax.ShapeDtypeStruct((m, D_MODEL), F32),
        scratch_shapes=[pltpu.VMEM((TOKEN_TILE + 2 * HALO_ROWS, CONV_WIDTH), F32),
                        pltpu.VMEM((TOKEN_TILE, D_MODEL), BF16)],
        compiler_params=pltpu.CompilerParams(
            dimension_semantics=("parallel",), vmem_limit_bytes=VMEM_LIMIT_BYTES),
        name="mix_out",
    )(x2, ygm, q, k, v, sg, hglu, hglu, hglu, sf, sb, dmask, xif, xib, cw, cb, clg, clb, wout)


def _ffn_body(final, x_ref, g2_ref, win_ref, wout_ref, gf_ref, o_ref):
    x = x_ref[...]
    h = _rmsnorm(x, g2_ref[...]).astype(BF16)
    acc = x
    for j in range(FFN_HIDDEN // FFN_COL_BLOCK):
        lo = j * FFN_COL_BLOCK
        gate = jnp.dot(h, win_ref[:, lo:lo + FFN_COL_BLOCK], preferred_element_type=F32)
        up = jnp.dot(h, win_ref[:, FFN_HIDDEN + lo:FFN_HIDDEN + lo + FFN_COL_BLOCK],
                     preferred_element_type=F32)
        act = (gate * _sigmoid(gate) * up).astype(BF16)
        acc = acc + jnp.dot(act, wout_ref[lo:lo + FFN_COL_BLOCK, :], preferred_element_type=F32)
    o_ref[...] = _rmsnorm(acc, gf_ref[...]) if final else acc


def _ffn(x2, g2, win, wout, gf, final):
    m = x2.shape[0]
    tok = pl.BlockSpec((TOKEN_TILE, D_MODEL), lambda t: (t, 0))
    const = lambda shape: pl.BlockSpec(shape, lambda t: (0,) * len(shape))
    return pl.pallas_call(
        functools.partial(_ffn_body, final),
        grid=(m // TOKEN_TILE,),
        in_specs=[tok, const((1, D_MODEL)), const((D_MODEL, 2 * FFN_HIDDEN)),
                  const((FFN_HIDDEN, D_MODEL)), const((1, D_MODEL))],
        out_specs=tok,
        out_shape=jax.ShapeDtypeStruct((m, D_MODEL), F32),
        compiler_params=pltpu.CompilerParams(
            dimension_semantics=("parallel",), vmem_limit_bytes=VMEM_LIMIT_BYTES),
        name="ffn_final" if final else "ffn",
    )(x2, g2, win, wout, gf)


def kernel(x, norm1_g, w_in, gm_ln_g, gm_ln_b, gm_ws, gm_bs, conv_w, conv_b, conv_ln_g, conv_ln_b,
           w_out, norm2_g, w_ffn_in, w_ffn_out, final_g):
    batch, seq, _ = x.shape
    depth = w_in.shape[0]
    assert seq % (STATE_CHUNKS * CHUNK) == 0 and seq % TOKEN_TILE == 0

    dmask, zf, zb, xif, xib, gc_f, gc_b = _retention_tables()
    cos2, sin2 = _rotary_tables(seq)
    row = lambda a: a.reshape(1, -1)

    x2 = x.reshape(batch * seq, D_MODEL)
    for l in range(depth):
        ws = gm_ws[l].reshape(GM_HEADS * CHUNK, CHUNK).astype(BF16)
        bs = jnp.repeat(gm_bs[l].T, GM_HEAD_DIM, axis=1)
        ygm, q, k, v, sg, hglu = _in_proj(
            x2, row(norm1_g[l]), w_in[l].astype(BF16), cos2, sin2,
            row(gm_ln_g[l]), row(gm_ln_b[l]), ws, bs, seq)
        sf, sb = _ret_state(k, v, zf, zb, gc_f, gc_b, batch, seq)
        x2 = _mix_out(x2, ygm, q, k, v, sg, hglu, sf, sb, dmask, xif, xib,
                      conv_w[l], row(conv_b[l]), row(conv_ln_g[l]), row(conv_ln_b[l]),
                      w_out[l].astype(BF16), seq)
        x2 = _ffn(x2, row(norm2_g[l]), w_ffn_in[l].astype(BF16), w_ffn_out[l].astype(BF16),
                  row(final_g), final=(l == depth - 1))
    return x2.reshape(batch, seq, D_MODEL)
```

```python
import functools

import numpy as np
import jax
import jax.numpy as jnp
from jax import lax
from jax.experimental import pallas as pl
from jax.experimental.pallas import tpu as pltpu

F32 = jnp.float32
BF16 = jnp.bfloat16

D_MODEL = 1024
GM_WIDTH = 256
GM_HEADS = 4
GM_HEAD_DIM = GM_WIDTH // GM_HEADS
CHUNK = 128
RET_WIDTH = 512
RET_HEADS = 4
RET_HEAD_DIM = RET_WIDTH // RET_HEADS
CONV_WIDTH = 256
CONV_KERNEL = 31
CONV_PAD = CONV_KERNEL // 2
IN_WIDTH = 2 * GM_WIDTH + 4 * RET_WIDTH + 2 * CONV_WIDTH
FFN_HIDDEN = 2816
ROPE_BASE = 10000.0
EPS = 1e-6

COL_GM = 0
COL_Q = 2 * GM_WIDTH
COL_V = COL_Q + 2 * RET_WIDTH
COL_CONV = COL_V + 2 * RET_WIDTH

TOKEN_TILE = 512
HALO_ROWS = 16
STATE_CHUNKS = 8
CONV_ROW_BLOCK = 64
SUBLANES = 8
FFN_COL_BLOCK = 256
VMEM_LIMIT_BYTES = 56 * 1024 * 1024


def _gelu(x):
    return 0.5 * x * (1.0 + lax.erf(x * np.float32(1.0 / np.sqrt(2.0))))


def _sigmoid(x):
    return 1.0 / (1.0 + jnp.exp(-x))


def _standardize(x):
    mu = jnp.mean(x, axis=-1, keepdims=True)
    xc = x - mu
    var = jnp.mean(xc * xc, axis=-1, keepdims=True)
    return xc * lax.rsqrt(var + EPS)


def _rmsnorm(x, g):
    return x * lax.rsqrt(jnp.mean(x * x, axis=-1, keepdims=True) + EPS) * g


def _retention_tables():
    idx = np.arange(CHUNK, dtype=np.float32)
    gamma_f = (1.0 - np.exp2(-5.0 - np.arange(RET_HEADS, dtype=np.float32))).astype(np.float32)
    gamma_b = gamma_f[::-1]
    lf = np.log(gamma_f)[:, None]
    lb = np.log(gamma_b)[:, None]
    diff = idx[:, None] - idx[None, :]
    dmask = np.where(diff >= 0,
                     np.exp(lf[:, :, None] * np.maximum(diff, 0.0)),
                     np.exp(lb[:, :, None] * np.maximum(-diff, 0.0))).astype(np.float32)
    zeta_f = np.exp(lf * (CHUNK - 1 - idx))
    zeta_b = np.exp(lb * idx)
    xi_f = np.exp(lf * (idx + 1))
    xi_b = np.exp(lb * (CHUNK - idx))

    def per_row(t):
        return np.repeat(t.T.astype(np.float32), RET_HEAD_DIM, axis=1)

    gc_f = [float(np.exp(np.float32(l) * np.float32(CHUNK))) for l in lf[:, 0]]
    gc_b = [float(np.exp(np.float32(l) * np.float32(CHUNK))) for l in lb[:, 0]]
    return dmask, per_row(zeta_f), per_row(zeta_b), per_row(xi_f), per_row(xi_b), gc_f, gc_b


def _rotary_tables(seq):
    half = RET_HEAD_DIM // 2
    inv_freq = ROPE_BASE ** (-np.arange(half, dtype=np.float64) / half)
    ang = np.arange(seq, dtype=np.float64)[:, None] * inv_freq[None, :]
    cos, sin = np.cos(ang), np.sin(ang)
    cos2 = np.concatenate([cos, cos], axis=-1).astype(np.float32)
    sin2 = np.concatenate([-sin, sin], axis=-1).astype(np.float32)
    return cos2, sin2


def _in_proj_body(x_ref, g1_ref, w_ref, cos_ref, sin_ref, lng_ref, lnb_ref, ws_ref, bs_ref,
                  ygm_ref, q_ref, k_ref, v_ref, sg_ref, hglu_ref):
    h = _rmsnorm(x_ref[...], g1_ref[...]).astype(BF16)

    def proj(lo, hi):
        return jnp.dot(h, w_ref[:, lo:hi], preferred_element_type=F32)

    uv = proj(COL_GM, COL_Q)
    u = _gelu(uv[:, :GM_WIDTH])
    v = _gelu(uv[:, GM_WIDTH:])
    vln = (_standardize(v) * lng_ref[...] + lnb_ref[...]).astype(BF16)
    lane_head = lax.broadcasted_iota(jnp.int32, (CHUNK, GM_WIDTH), 1) // GM_HEAD_DIM
    for c in range(TOKEN_TILE // CHUNK):
        rows = slice(c * CHUNK, (c + 1) * CHUNK)
        full = jnp.dot(ws_ref[...], vln[rows], preferred_element_type=F32)
        mixed = full[0:CHUNK]
        for hd in range(1, GM_HEADS):
            mixed = jnp.where(lane_head == hd, full[hd * CHUNK:(hd + 1) * CHUNK], mixed)
        ygm_ref[rows, :] = (u[rows] * (mixed + bs_ref[...])).astype(BF16)

    cos2 = cos_ref[...]
    sin2 = sin_ref[...]
    qk = proj(COL_Q, COL_V)
    scale = np.float32(RET_HEAD_DIM ** -0.5)
    for hd in range(RET_HEADS):
        cols = slice(hd * RET_HEAD_DIM, (hd + 1) * RET_HEAD_DIM)
        t = qk[:, cols]
        q_ref[:, cols] = (t * cos2 + pltpu.roll(t, RET_HEAD_DIM // 2, 1) * sin2).astype(BF16)
        t = qk[:, RET_WIDTH + hd * RET_HEAD_DIM:RET_WIDTH + (hd + 1) * RET_HEAD_DIM]
        k_ref[:, cols] = ((t * cos2 + pltpu.roll(t, RET_HEAD_DIM // 2, 1) * sin2) * scale).astype(BF16)
    vg = proj(COL_V, COL_CONV)
    v_ref[...] = vg[:, :RET_WIDTH].astype(BF16)
    g = vg[:, RET_WIDTH:]
    sg_ref[...] = (g * _sigmoid(g)).astype(BF16)

    ag = proj(COL_CONV, IN_WIDTH)
    hglu_ref[...] = ag[:, :CONV_WIDTH] * _sigmoid(ag[:, CONV_WIDTH:])


def _layer_weight(layer, rows, cols):
    return pl.BlockSpec((None, rows, cols), lambda t: (layer, 0, 0))


def _in_proj(x2, g1, w_in, layer, cos2, sin2, lng, lnb, ws, bs, seq):
    m = x2.shape[0]
    tiles_per_seq = seq // TOKEN_TILE
    tok = lambda width: pl.BlockSpec((TOKEN_TILE, width), lambda t: (t, 0))
    const = lambda shape: pl.BlockSpec(shape, lambda t: (0,) * len(shape))
    return pl.pallas_call(
        _in_proj_body,
        grid=(m // TOKEN_TILE,),
        in_specs=[
            tok(D_MODEL), const((1, D_MODEL)), _layer_weight(layer, D_MODEL, IN_WIDTH),
            pl.BlockSpec((TOKEN_TILE, RET_HEAD_DIM), lambda t: (t % tiles_per_seq, 0)),
            pl.BlockSpec((TOKEN_TILE, RET_HEAD_DIM), lambda t: (t % tiles_per_seq, 0)),
            const((1, GM_WIDTH)), const((1, GM_WIDTH)),
            const((GM_HEADS * CHUNK, CHUNK)), const((CHUNK, GM_WIDTH)),
        ],
        out_specs=[tok(GM_WIDTH), tok(RET_WIDTH), tok(RET_WIDTH), tok(RET_WIDTH), tok(RET_WIDTH),
                   tok(CONV_WIDTH)],
        out_shape=[
            jax.ShapeDtypeStruct((m, GM_WIDTH), BF16),
            jax.ShapeDtypeStruct((m, RET_WIDTH), BF16),
            jax.ShapeDtypeStruct((m, RET_WIDTH), BF16),
            jax.ShapeDtypeStruct((m, RET_WIDTH), BF16),
            jax.ShapeDtypeStruct((m, RET_WIDTH), BF16),
            jax.ShapeDtypeStruct((m, CONV_WIDTH), F32),
        ],
        compiler_params=pltpu.CompilerParams(
            dimension_semantics=("parallel",), vmem_limit_bytes=VMEM_LIMIT_BYTES),
        name="in_proj",
    )(x2, g1, w_in, cos2, sin2, lng, lnb, ws, bs)


def _ret_state_body(gc_f, gc_b, kf_ref, vf_ref, kb_ref, vb_ref, zf_ref, zb_ref,
                    sf_ref, sb_ref, st_ref):
    @pl.when(pl.program_id(1) == 0)
    def _():
        st_ref[...] = jnp.zeros_like(st_ref)

    def chunk_kv(k_ref, v_ref, z_ref, j):
        rows = slice(j * CHUNK, (j + 1) * CHUNK)
        vz = (v_ref[rows, :].astype(F32) * z_ref[...]).astype(BF16)
        out = []
        for hd in range(RET_HEADS):
            cols = slice(hd * RET_HEAD_DIM, (hd + 1) * RET_HEAD_DIM)
            out.append(lax.dot_general(k_ref[rows, cols], vz[:, cols], (((0,), (0,)), ((), ())),
                                       preferred_element_type=F32))
        return out

    for d, (k_ref, v_ref, z_ref, s_ref, gc, order) in enumerate((
            (kf_ref, vf_ref, zf_ref, sf_ref, gc_f, range(STATE_CHUNKS)),
            (kb_ref, vb_ref, zb_ref, sb_ref, gc_b, range(STATE_CHUNKS - 1, -1, -1)))):
        for j in order:
            kv = chunk_kv(k_ref, v_ref, z_ref, j)
            for hd in range(RET_HEADS):
                state = st_ref[d, hd]
                s_ref[j, hd] = state.astype(BF16)
                st_ref[d, hd] = gc[hd] * state + kv[hd]


def _ret_state(k, v, zf, zb, gc_f, gc_b, batch, seq):
    rows = STATE_CHUNKS * CHUNK
    nb = seq // rows
    n_chunks = batch * seq // CHUNK
    fwd = pl.BlockSpec((rows, RET_WIDTH), lambda b, i: (b * nb + i, 0))
    bwd = pl.BlockSpec((rows, RET_WIDTH), lambda b, i: (b * nb + nb - 1 - i, 0))
    const = pl.BlockSpec((CHUNK, RET_WIDTH), lambda b, i: (0, 0))
    state_shape = (STATE_CHUNKS, RET_HEADS, RET_HEAD_DIM, RET_HEAD_DIM)
    return pl.pallas_call(
        functools.partial(_ret_state_body, gc_f, gc_b),
        grid=(batch, nb),
        in_specs=[fwd, fwd, bwd, bwd, const, const],
        out_specs=[
            pl.BlockSpec(state_shape, lambda b, i: (b * nb + i, 0, 0, 0)),
            pl.BlockSpec(state_shape, lambda b, i: (b * nb + nb - 1 - i, 0, 0, 0)),
        ],
        out_shape=[jax.ShapeDtypeStruct((n_chunks, RET_HEADS, RET_HEAD_DIM, RET_HEAD_DIM), BF16)] * 2,
        scratch_shapes=[pltpu.VMEM((2, RET_HEADS, RET_HEAD_DIM, RET_HEAD_DIM), F32)],
        compiler_params=pltpu.CompilerParams(
            dimension_semantics=("parallel", "arbitrary"), vmem_limit_bytes=VMEM_LIMIT_BYTES),
        name="ret_state",
    )(k, v, k, v, zf, zb)


def _mix_out_body(tiles_per_seq, x_ref, ygm_ref, q_ref, k_ref, v_ref, sg_ref,
                  hmain_ref, hprev_ref, hnext_ref, sf_ref, sb_ref, dmask_ref, xif_ref, xib_ref,
                  cw_ref, cb_ref, clg_ref, clb_ref, wout_ref, o_ref, hpad_ref, y_ref):
    t = pl.program_id(0)
    first = (t % tiles_per_seq) == 0
    last = (t % tiles_per_seq) == tiles_per_seq - 1

    y_ref[:, 0:GM_WIDTH] = ygm_ref[...]

    for c in range(TOKEN_TILE // CHUNK):
        rows = slice(c * CHUNK, (c + 1) * CHUNK)
        for hd in range(RET_HEADS):
            cols = slice(hd * RET_HEAD_DIM, (hd + 1) * RET_HEAD_DIM)
            qh = q_ref[rows, cols]
            scores = lax.dot_general(qh, k_ref[rows, cols], (((1,), (1,)), ((), ())),
                                     preferred_element_type=F32)
            p = (scores * dmask_ref[hd]).astype(BF16)
            o = jnp.dot(p, v_ref[rows, cols], preferred_element_type=F32)
            o = o + xif_ref[:, cols] * jnp.dot(qh, sf_ref[c, hd], preferred_element_type=F32)
            o = o + xib_ref[:, cols] * jnp.dot(qh, sb_ref[c, hd], preferred_element_type=F32)
            y_ref[rows, GM_WIDTH + hd * RET_HEAD_DIM:GM_WIDTH + (hd + 1) * RET_HEAD_DIM] = (
                _standardize(o) * sg_ref[rows, cols].astype(F32)).astype(BF16)

    zero_halo = jnp.zeros((HALO_ROWS, CONV_WIDTH), F32)
    hpad_ref[0:HALO_ROWS, :] = jnp.where(first, zero_halo, hprev_ref[...])
    hpad_ref[HALO_ROWS:HALO_ROWS + TOKEN_TILE, :] = hmain_ref[...]
    hpad_ref[HALO_ROWS + TOKEN_TILE:, :] = jnp.where(last, zero_halo, hnext_ref[...])
    span = CONV_ROW_BLOCK + SUBLANES
    for rb in range(TOKEN_TILE // CONV_ROW_BLOCK):
        r0 = rb * CONV_ROW_BLOCK
        acc = None
        for r in range(SUBLANES):
            part = None
            for j in range(CONV_KERNEL):
                if (j - CONV_PAD) % SUBLANES != r:
                    continue
                start = r0 + HALO_ROWS + (j - CONV_PAD - r)
                term = cw_ref[j:j + 1, :] * hpad_ref[start:start + span, :]
                part = term if part is None else part + term
            if r:
                part = pltpu.roll(part, span - r, 0)
            part = part[:CONV_ROW_BLOCK]
            acc = part if acc is None else acc + part
        hc =_standardize(acc + cb_ref[...]) * clg_ref[...] + clb_ref[...]
        y_ref[r0:r0 + CONV_ROW_BLOCK, GM_WIDTH + RET_WIDTH:] = (hc * _sigmoid(hc)).astype(BF16)

    o_ref[...] = x_ref[...] + jnp.dot(y_ref[...], wout_ref[...], preferred_element_type=F32)


def _mix_out(x2, ygm, q, k, v, sg, hglu, sf, sb, dmask, xif, xib, cw, cb, clg, clb, wout, layer, seq):
    m = x2.shape[0]
    tiles_per_seq = seq // TOKEN_TILE
    halo_per_tile = TOKEN_TILE // HALO_ROWS
    n_halo = m // HALO_ROWS
    tok = lambda width: pl.BlockSpec((TOKEN_TILE, width), lambda t: (t, 0))
    const = lambda shape: pl.BlockSpec(shape, lambda t: (0,) * len(shape))
    state = pl.BlockSpec((TOKEN_TILE // CHUNK, RET_HEADS, RET_HEAD_DIM, RET_HEAD_DIM),
                         lambda t: (t, 0, 0, 0))
    return pl.pallas_call(
        functools.partial(_mix_out_body, tiles_per_seq),
        grid=(m // TOKEN_TILE,),
        in_specs=[
            tok(D_MODEL), tok(GM_WIDTH), tok(RET_WIDTH), tok(RET_WIDTH), tok(RET_WIDTH), tok(RET_WIDTH),
            tok(CONV_WIDTH),
            pl.BlockSpec((HALO_ROWS, CONV_WIDTH), lambda t: (jnp.maximum(t * halo_per_tile - 1, 0), 0)),
            pl.BlockSpec((HALO_ROWS, CONV_WIDTH),
                         lambda t: (jnp.minimum((t + 1) * halo_per_tile, n_halo - 1), 0)),
            state, state,
            const((RET_HEADS, CHUNK, CHUNK)), const((CHUNK, RET_WIDTH)), const((CHUNK, RET_WIDTH)),
            const((CONV_KERNEL, CONV_WIDTH)), const((1, CONV_WIDTH)), const((1, CONV_WIDTH)),
            const((1, CONV_WIDTH)), _layer_weight(layer, D_MODEL, D_MODEL),
        ],
        out_specs=tok(D_MODEL),
        out_shape=jax.ShapeDtypeStruct((m, D_MODEL), F32),
        scratch_shapes=[pltpu.VMEM((TOKEN_TILE + 2 * HALO_ROWS, CONV_WIDTH), F32),
                        pltpu.VMEM((TOKEN_TILE, D_MODEL), BF16)],
        compiler_params=pltpu.CompilerParams(
            dimension_semantics=("parallel",), vmem_limit_bytes=VMEM_LIMIT_BYTES),
        name="mix_out",
    )(x2, ygm, q, k, v, sg, hglu, hglu, hglu, sf, sb, dmask, xif, xib, cw, cb, clg, clb, wout)


def _ffn_body(final, x_ref, g2_ref, win_ref, wout_ref, gf_ref, o_ref):
    x = x_ref[...]
    h = _rmsnorm(x, g2_ref[...]).astype(BF16)
    acc = x
    for j in range(FFN_HIDDEN // FFN_COL_BLOCK):
        lo = j * FFN_COL_BLOCK
        gate = jnp.dot(h, win_ref[:, lo:lo + FFN_COL_BLOCK], preferred_element_type=F32)
        up = jnp.dot(h, win_ref[:, FFN_HIDDEN + lo:FFN_HIDDEN + lo + FFN_COL_BLOCK],
                     preferred_element_type=F32)
        act = (gate * _sigmoid(gate) * up).astype(BF16)
        acc = acc + jnp.dot(act, wout_ref[lo:lo + FFN_COL_BLOCK, :], preferred_element_type=F32)
    o_ref[...] = _rmsnorm(acc, gf_ref[...]) if final else acc


def _ffn(x2, g2, win, wout, layer, gf, final):
    m = x2.shape[0]
    tok = pl.BlockSpec((TOKEN_TILE, D_MODEL), lambda t: (t, 0))
    const = lambda shape: pl.BlockSpec(shape, lambda t: (0,) * len(shape))
    return pl.pallas_call(
        functools.partial(_ffn_body, final),
        grid=(m // TOKEN_TILE,),
        in_specs=[tok, const((1, D_MODEL)), _layer_weight(layer, D_MODEL, 2 * FFN_HIDDEN),
                  _layer_weight(layer, FFN_HIDDEN, D_MODEL), const((1, D_MODEL))],
        out_specs=tok,
        out_shape=jax.ShapeDtypeStruct((m, D_MODEL), F32),
        compiler_params=pltpu.CompilerParams(
            dimension_semantics=("parallel",), vmem_limit_bytes=VMEM_LIMIT_BYTES),
        name="ffn_final" if final else "ffn",
    )(x2, g2, win, wout, gf)


def kernel(x, norm1_g, w_in, gm_ln_g, gm_ln_b, gm_ws, gm_bs, conv_w, conv_b, conv_ln_g, conv_ln_b,
           w_out, norm2_g, w_ffn_in, w_ffn_out, final_g):
    batch, seq, _ = x.shape
    depth = w_in.shape[0]
    assert seq % (STATE_CHUNKS * CHUNK) == 0 and seq % TOKEN_TILE == 0

    dmask, zf, zb, xif, xib, gc_f, gc_b = _retention_tables()
    cos2, sin2 = _rotary_tables(seq)
    row = lambda a: a.reshape(1, -1)

    w_in_b, w_out_b = w_in.astype(BF16), w_out.astype(BF16)
    w_ffn_in_b, w_ffn_out_b = w_ffn_in.astype(BF16), w_ffn_out.astype(BF16)

    x2 = x.reshape(batch * seq, D_MODEL)
    for l in range(depth):
        ws = gm_ws[l].reshape(GM_HEADS * CHUNK, CHUNK).astype(BF16)
        bs = jnp.repeat(gm_bs[l].T, GM_HEAD_DIM, axis=1)
        ygm, q, k, v, sg, hglu = _in_proj(
            x2, row(norm1_g[l]), w_in_b, l, cos2, sin2,
            row(gm_ln_g[l]), row(gm_ln_b[l]), ws, bs, seq)
        sf, sb = _ret_state(k, v, zf, zb, gc_f, gc_b, batch, seq)
        x2 = _mix_out(x2, ygm, q, k, v, sg, hglu, sf, sb, dmask, xif, xib,
                      conv_w[l], row(conv_b[l]), row(conv_ln_g[l]), row(conv_ln_b[l]),
                      w_out_b, l, seq)
        x2 = _ffn(x2, row(norm2_g[l]), w_ffn_in_b, w_ffn_out_b, l,
                  row(final_g), final=(l == depth - 1))
    return x2.reshape(batch, seq, D_MODEL)
```

```python
import functools

import numpy as np
import jax
import jax.numpy as jnp
from jax import lax
from jax.experimental import pallas as pl
from jax.experimental.pallas import tpu as pltpu

F32 = jnp.float32
BF16 = jnp.bfloat16

D_MODEL = 1024
GM_WIDTH = 256
GM_HEADS = 4
GM_HEAD_DIM = GM_WIDTH // GM_HEADS
CHUNK = 128
RET_WIDTH = 512
RET_HEADS = 4
RET_HEAD_DIM = RET_WIDTH // RET_HEADS
CONV_WIDTH = 256
CONV_KERNEL = 31
CONV_PAD = CONV_KERNEL // 2
IN_WIDTH = 2 * GM_WIDTH + 4 * RET_WIDTH + 2 * CONV_WIDTH
FFN_HIDDEN = 2816
ROPE_BASE = 10000.0
EPS = 1e-6

COL_GM = 0
COL_Q = 2 * GM_WIDTH
COL_V = COL_Q + 2 * RET_WIDTH
COL_CONV = COL_V + 2 * RET_WIDTH

TOKEN_TILE = 512
HALO_ROWS = 16
STATE_CHUNKS = 8
CONV_ROW_BLOCK = 64
SUBLANES = 8
FFN_COL_BLOCK = 256
OUT_COL_BLOCK = 256
VMEM_LIMIT_BYTES = 56 * 1024 * 1024


def _gelu(x):
    return 0.5 * x * (1.0 + lax.erf(x * np.float32(1.0 / np.sqrt(2.0))))


def _sigmoid(x):
    return 1.0 / (1.0 + jnp.exp(-x))


def _standardize(x):
    mu = jnp.mean(x, axis=-1, keepdims=True)
    xc = x - mu
    var = jnp.mean(xc * xc, axis=-1, keepdims=True)
    return xc * lax.rsqrt(var + EPS)


def _rmsnorm(x, g):
    return x * lax.rsqrt(jnp.mean(x * x, axis=-1, keepdims=True) + EPS) * g


def _retention_tables():
    idx = np.arange(CHUNK, dtype=np.float32)
    gamma_f = (1.0 - np.exp2(-5.0 - np.arange(RET_HEADS, dtype=np.float32))).astype(np.float32)
    gamma_b = gamma_f[::-1]
    lf = np.log(gamma_f)[:, None]
    lb = np.log(gamma_b)[:, None]
    diff = idx[:, None] - idx[None, :]
    dmask = np.where(diff >= 0,
                     np.exp(lf[:, :, None] * np.maximum(diff, 0.0)),
                     np.exp(lb[:, :, None] * np.maximum(-diff, 0.0))).astype(np.float32)
    zeta_f = np.exp(lf * (CHUNK - 1 - idx))
    zeta_b = np.exp(lb * idx)
    xi_f = np.exp(lf * (idx + 1))
    xi_b = np.exp(lb * (CHUNK - idx))

    def per_row(t):
        return np.repeat(t.T.astype(np.float32), RET_HEAD_DIM, axis=1)

    gc_f = [float(np.exp(np.float32(l) * np.float32(CHUNK))) for l in lf[:, 0]]
    gc_b = [float(np.exp(np.float32(l) * np.float32(CHUNK))) for l in lb[:, 0]]
    return dmask, per_row(zeta_f), per_row(zeta_b), per_row(xi_f), per_row(xi_b), gc_f, gc_b


def _rotary_tables(seq):
    half = RET_HEAD_DIM // 2
    inv_freq = ROPE_BASE ** (-np.arange(half, dtype=np.float64) / half)
    ang = np.arange(seq, dtype=np.float64)[:, None] * inv_freq[None, :]
    cos, sin = np.cos(ang), np.sin(ang)
    cos2 = np.concatenate([cos, cos], axis=-1).astype(np.float32)
    sin2 = np.concatenate([-sin, sin], axis=-1).astype(np.float32)
    return cos2, sin2


def _layer_weight(layer, rows, cols):
    return pl.BlockSpec((None, rows, cols), lambda t: (layer, 0, 0))


def _in_proj_body(x_ref, g1_ref, w_ref, cos_ref, sin_ref, lng_ref, lnb_ref, ws_ref, bs_ref,
                  ygm_ref, q_ref, k_ref, v_ref, sg_ref, hglu_ref):
    h = _rmsnorm(x_ref[...], g1_ref[...]).astype(BF16)

    def proj(lo, hi):
        return jnp.dot(h, w_ref[:, lo:hi], preferred_element_type=F32)

    uv = proj(COL_GM, COL_Q)
    u = _gelu(uv[:, :GM_WIDTH])
    v = _gelu(uv[:, GM_WIDTH:])
    vln = (_standardize(v) * lng_ref[...] + lnb_ref[...]).astype(BF16)
    lane_head = lax.broadcasted_iota(jnp.int32, (CHUNK, GM_WIDTH), 1) // GM_HEAD_DIM
    for c in range(TOKEN_TILE // CHUNK):
        rows = slice(c * CHUNK, (c + 1) * CHUNK)
        full = jnp.dot(ws_ref[...], vln[rows], preferred_element_type=F32)
        mixed = full[0:CHUNK]
        for hd in range(1, GM_HEADS):
            mixed = jnp.where(lane_head == hd, full[hd * CHUNK:(hd + 1) * CHUNK], mixed)
        ygm_ref[rows, :] = (u[rows] * (mixed + bs_ref[...])).astype(BF16)

    cos2 = cos_ref[...]
    sin2 = sin_ref[...]
    qk = proj(COL_Q, COL_V)
    scale = np.float32(RET_HEAD_DIM ** -0.5)
    for hd in range(RET_HEADS):
        cols = slice(hd * RET_HEAD_DIM, (hd + 1) * RET_HEAD_DIM)
        t = qk[:, cols]
        q_ref[:, cols] = (t * cos2 + pltpu.roll(t, RET_HEAD_DIM // 2, 1) * sin2).astype(BF16)
        t = qk[:, RET_WIDTH + hd * RET_HEAD_DIM:RET_WIDTH + (hd + 1) * RET_HEAD_DIM]
        k_ref[:, cols] = ((t * cos2 + pltpu.roll(t, RET_HEAD_DIM // 2, 1) * sin2) * scale).astype(BF16)
    vg = proj(COL_V, COL_CONV)
    v_ref[...] = vg[:, :RET_WIDTH].astype(BF16)
    g = vg[:, RET_WIDTH:]
    sg_ref[...] = (g * _sigmoid(g)).astype(BF16)

    ag = proj(COL_CONV, IN_WIDTH)
    hglu_ref[...] = ag[:, :CONV_WIDTH] * _sigmoid(ag[:, CONV_WIDTH:])


def _in_proj(x2, g1, w_in, layer, cos2, sin2, lng, lnb, ws, bs, seq):
    m = x2.shape[0]
    tiles_per_seq = seq // TOKEN_TILE
    tok = lambda width: pl.BlockSpec((TOKEN_TILE, width), lambda t: (t, 0))
    const = lambda shape: pl.BlockSpec(shape, lambda t: (0,) * len(shape))
    return pl.pallas_call(
        _in_proj_body,
        grid=(m // TOKEN_TILE,),
        in_specs=[
            tok(D_MODEL), const((1, D_MODEL)), _layer_weight(layer, D_MODEL, IN_WIDTH),
            pl.BlockSpec((TOKEN_TILE, RET_HEAD_DIM), lambda t: (t % tiles_per_seq, 0)),
            pl.BlockSpec((TOKEN_TILE, RET_HEAD_DIM), lambda t: (t % tiles_per_seq, 0)),
            const((1, GM_WIDTH)), const((1, GM_WIDTH)),
            const((GM_HEADS * CHUNK, CHUNK)), const((CHUNK, GM_WIDTH)),
        ],
        out_specs=[tok(GM_WIDTH), tok(RET_WIDTH), tok(RET_WIDTH), tok(RET_WIDTH), tok(RET_WIDTH),
                   tok(CONV_WIDTH)],
        out_shape=[
            jax.ShapeDtypeStruct((m, GM_WIDTH), BF16),
            jax.ShapeDtypeStruct((m, RET_WIDTH), BF16),
            jax.ShapeDtypeStruct((m, RET_WIDTH), BF16),
            jax.ShapeDtypeStruct((m, RET_WIDTH), BF16),
            jax.ShapeDtypeStruct((m, RET_WIDTH), BF16),
            jax.ShapeDtypeStruct((m, CONV_WIDTH), F32),
        ],
        compiler_params=pltpu.CompilerParams(
            dimension_semantics=("parallel",), vmem_limit_bytes=VMEM_LIMIT_BYTES),
        name="in_proj",
    )(x2, g1, w_in, cos2, sin2, lng, lnb, ws, bs)


def _ret_state_body(gc_f, gc_b, kf_ref, vf_ref, kb_ref, vb_ref, zf_ref, zb_ref,
                    sf_ref, sb_ref, st_ref):
    @pl.when(pl.program_id(1) == 0)
    def _():
        st_ref[...] = jnp.zeros_like(st_ref)

    def chunk_kv(k_ref, v_ref, z_ref, j):
        rows = slice(j * CHUNK, (j + 1) * CHUNK)
        vz = (v_ref[rows, :].astype(F32) * z_ref[...]).astype(BF16)
        out = []
        for hd in range(RET_HEADS):
            cols = slice(hd * RET_HEAD_DIM, (hd + 1) * RET_HEAD_DIM)
            out.append(lax.dot_general(k_ref[rows, cols], vz[:, cols], (((0,), (0,)), ((), ())),
                                       preferred_element_type=F32))
        return out

    for d, (k_ref, v_ref, z_ref, s_ref, gc, order) in enumerate((
            (kf_ref, vf_ref, zf_ref, sf_ref, gc_f, range(STATE_CHUNKS)),
            (kb_ref, vb_ref, zb_ref, sb_ref, gc_b, range(STATE_CHUNKS - 1, -1, -1)))):
        for j in order:
            kv = chunk_kv(k_ref, v_ref, z_ref, j)
            for hd in range(RET_HEADS):
                state = st_ref[d, hd]
                s_ref[j, hd] = state.astype(BF16)
                st_ref[d, hd] = gc[hd] * state + kv[hd]


def _ret_state(k, v, zf, zb, gc_f, gc_b, batch, seq):
    rows = STATE_CHUNKS * CHUNK
    nb = seq // rows
    n_chunks = batch * seq // CHUNK
    fwd = pl.BlockSpec((rows, RET_WIDTH), lambda b, i: (b * nb + i, 0))
    bwd = pl.BlockSpec((rows, RET_WIDTH), lambda b, i: (b * nb + nb - 1 - i, 0))
    const = pl.BlockSpec((CHUNK, RET_WIDTH), lambda b, i: (0, 0))
    state_shape = (STATE_CHUNKS, RET_HEADS, RET_HEAD_DIM, RET_HEAD_DIM)
    return pl.pallas_call(
        functools.partial(_ret_state_body, gc_f, gc_b),
        grid=(batch, nb),
        in_specs=[fwd, fwd, bwd, bwd, const, const],
        out_specs=[
            pl.BlockSpec(state_shape, lambda b, i: (b * nb + i, 0, 0, 0)),
            pl.BlockSpec(state_shape, lambda b, i: (b * nb + nb - 1 - i, 0, 0, 0)),
        ],
        out_shape=[jax.ShapeDtypeStruct((n_chunks, RET_HEADS, RET_HEAD_DIM, RET_HEAD_DIM), BF16)] * 2,
        scratch_shapes=[pltpu.VMEM((2, RET_HEADS, RET_HEAD_DIM, RET_HEAD_DIM), F32)],
        compiler_params=pltpu.CompilerParams(
            dimension_semantics=("parallel", "arbitrary"), vmem_limit_bytes=VMEM_LIMIT_BYTES),
        name="ret_state",
    )(k, v, k, v, zf, zb)


def _mix_out_body(tiles_per_seq, x_ref, ygm_ref, q_ref, k_ref, v_ref, sg_ref,
                  hmain_ref, hprev_ref, hnext_ref, sf_ref, sb_ref, dmask_ref, xif_ref, xib_ref,
                  cw_ref, cb_ref, clg_ref, clb_ref, wout_ref, o_ref,
                  hpad_ref, p_ref, oret_ref, yret_ref, ycv_ref):
    t = pl.program_id(0)
    first = (t % tiles_per_seq) == 0
    last = (t % tiles_per_seq) == tiles_per_seq - 1

    pairs = [(c, hd) for c in range(TOKEN_TILE // CHUNK) for hd in range(RET_HEADS)]

    def block(c, hd):
        return slice(c * CHUNK, (c + 1) * CHUNK), slice(hd * RET_HEAD_DIM, (hd + 1) * RET_HEAD_DIM)

    def masked_scores(i):
        c, hd = pairs[i]
        rows, cols = block(c, hd)
        scores = lax.dot_general(q_ref[rows, cols], k_ref[rows, cols], (((1,), (1,)), ((), ())),
                                 preferred_element_type=F32)
        p_ref[i] = (scores * dmask_ref[hd]).astype(BF16)

    def weighted_values(i):
        c, hd = pairs[i]
        rows, cols = block(c, hd)
        qh = q_ref[rows, cols]
        o = jnp.dot(p_ref[i], v_ref[rows, cols], preferred_element_type=F32)
        o = o + xif_ref[:, cols] * jnp.dot(qh, sf_ref[c, hd], preferred_element_type=F32)
        o = o + xib_ref[:, cols] * jnp.dot(qh, sb_ref[c, hd], preferred_element_type=F32)
        oret_ref[rows, cols] = o

    def head_norm_gate(hd):
        cols = slice(hd * RET_HEAD_DIM, (hd + 1) * RET_HEAD_DIM)
        yret_ref[:, cols] = (_standardize(oret_ref[:, cols])
                             * sg_ref[:, cols].astype(F32)).astype(BF16)

    def conv(rb):
        span = CONV_ROW_BLOCK + SUBLANES
        r0 = rb * CONV_ROW_BLOCK
        acc = None
        for r in range(SUBLANES):
            part = None
            for j in range(CONV_KERNEL):
                if (j - CONV_PAD) % SUBLANES != r:
                    continue
                start = r0 + HALO_ROWS + (j - CONV_PAD - r)
                term = cw_ref[j:j + 1, :] * hpad_ref[start:start + span, :]
                part = term if part is None else part + term
            if r:
                part = pltpu.roll(part, span - r, 0)
            part = part[:CONV_ROW_BLOCK]
            acc = part if acc is None else acc + part
        hc = _standardize(acc + cb_ref[...]) * clg_ref[...] + clb_ref[...]
        ycv_ref[r0:r0 + CONV_ROW_BLOCK, :] = (hc * _sigmoid(hc)).astype(BF16)

    zero_halo = jnp.zeros((HALO_ROWS, CONV_WIDTH), F32)
    hpad_ref[0:HALO_ROWS, :] = jnp.where(first, zero_halo, hprev_ref[...])
    hpad_ref[HALO_ROWS:HALO_ROWS + TOKEN_TILE, :] = hmain_ref[...]
    hpad_ref[HALO_ROWS + TOKEN_TILE:, :] = jnp.where(last, zero_halo, hnext_ref[...])

    def project_gm():
        o_ref[...] = x_ref[...] + jnp.dot(ygm_ref[...], wout_ref[0:GM_WIDTH, :],
                                          preferred_element_type=F32)

    def project_ret(n):
        cols = slice(n * OUT_COL_BLOCK, (n + 1) * OUT_COL_BLOCK)
        o_ref[:, cols] += jnp.dot(yret_ref[...], wout_ref[GM_WIDTH:GM_WIDTH + RET_WIDTH, cols],
                                  preferred_element_type=F32)

    project_gm()
    for i in range(len(pairs)):
        masked_scores(i)
    for i in range(len(pairs)):
        weighted_values(i)
    for hd in range(RET_HEADS):
        head_norm_gate(hd)
    for rb in range(TOKEN_TILE // CONV_ROW_BLOCK):
        conv(rb)
    for n in range(D_MODEL // OUT_COL_BLOCK):
        project_ret(n)
    o_ref[...] +=jnp.dot(ycv_ref[...], wout_ref[GM_WIDTH + RET_WIDTH:, :],
                          preferred_element_type=F32)


def _mix_out(x2, ygm, q, k, v, sg, hglu, sf, sb, dmask, xif, xib, cw, cb, clg, clb, wout, layer, seq):
    m = x2.shape[0]
    tiles_per_seq = seq // TOKEN_TILE
    halo_per_tile = TOKEN_TILE // HALO_ROWS
    n_halo = m // HALO_ROWS
    tok = lambda width: pl.BlockSpec((TOKEN_TILE, width), lambda t: (t, 0))
    const = lambda shape: pl.BlockSpec(shape, lambda t: (0,) * len(shape))
    state = pl.BlockSpec((TOKEN_TILE // CHUNK, RET_HEADS, RET_HEAD_DIM, RET_HEAD_DIM),
                         lambda t: (t, 0, 0, 0))
    return pl.pallas_call(
        functools.partial(_mix_out_body, tiles_per_seq),
        grid=(m // TOKEN_TILE,),
        in_specs=[
            tok(D_MODEL), tok(GM_WIDTH), tok(RET_WIDTH), tok(RET_WIDTH), tok(RET_WIDTH), tok(RET_WIDTH),
            tok(CONV_WIDTH),
            pl.BlockSpec((HALO_ROWS, CONV_WIDTH), lambda t: (jnp.maximum(t * halo_per_tile - 1, 0), 0)),
            pl.BlockSpec((HALO_ROWS, CONV_WIDTH),
                         lambda t: (jnp.minimum((t + 1) * halo_per_tile, n_halo - 1), 0)),
            state, state,
            const((RET_HEADS, CHUNK, CHUNK)), const((CHUNK, RET_WIDTH)), const((CHUNK, RET_WIDTH)),
            const((CONV_KERNEL, CONV_WIDTH)), const((1, CONV_WIDTH)), const((1, CONV_WIDTH)),
            const((1, CONV_WIDTH)), _layer_weight(layer, D_MODEL, D_MODEL),
        ],
        out_specs=tok(D_MODEL),
        out_shape=jax.ShapeDtypeStruct((m, D_MODEL), F32),
        scratch_shapes=[pltpu.VMEM((TOKEN_TILE + 2 * HALO_ROWS, CONV_WIDTH), F32),
                        pltpu.VMEM((TOKEN_TILE // CHUNK * RET_HEADS, CHUNK, CHUNK), BF16),
                        pltpu.VMEM((TOKEN_TILE, RET_WIDTH), F32),
                        pltpu.VMEM((TOKEN_TILE, RET_WIDTH), BF16),
                        pltpu.VMEM((TOKEN_TILE, CONV_WIDTH), BF16)],
        compiler_params=pltpu.CompilerParams(
            dimension_semantics=("parallel",), vmem_limit_bytes=VMEM_LIMIT_BYTES),
        name="mix_out",
    )(x2, ygm, q, k, v, sg, hglu, hglu, hglu, sf, sb, dmask, xif, xib, cw, cb, clg, clb, wout)


def _ffn_body(final, x_ref, g2_ref, win_ref, wout_ref, gf_ref, o_ref):
    x = x_ref[...]
    h = _rmsnorm(x, g2_ref[...]).astype(BF16)
    acc = x
    for j in range(FFN_HIDDEN // FFN_COL_BLOCK):
        lo = j * FFN_COL_BLOCK
        gate = jnp.dot(h, win_ref[:, lo:lo + FFN_COL_BLOCK], preferred_element_type=F32)
        up = jnp.dot(h, win_ref[:, FFN_HIDDEN + lo:FFN_HIDDEN + lo + FFN_COL_BLOCK],
                     preferred_element_type=F32)
        act = (gate * _sigmoid(gate) * up).astype(BF16)
        acc = acc + jnp.dot(act, wout_ref[lo:lo + FFN_COL_BLOCK, :], preferred_element_type=F32)
    o_ref[...] = _rmsnorm(acc, gf_ref[...]) if final else acc


def _ffn(x2, g2, win, wout, layer, gf, final):
    m = x2.shape[0]
    tok = pl.BlockSpec((TOKEN_TILE, D_MODEL), lambda t: (t, 0))
    const = lambda shape: pl.BlockSpec(shape, lambda t: (0,) * len(shape))
    return pl.pallas_call(
        functools.partial(_ffn_body, final),
        grid=(m // TOKEN_TILE,),
        in_specs=[tok, const((1, D_MODEL)), _layer_weight(layer, D_MODEL, 2 * FFN_HIDDEN),
                  _layer_weight(layer, FFN_HIDDEN, D_MODEL), const((1, D_MODEL))],
        out_specs=tok,
        out_shape=jax.ShapeDtypeStruct((m, D_MODEL), F32),
        compiler_params=pltpu.CompilerParams(
            dimension_semantics=("parallel",), vmem_limit_bytes=VMEM_LIMIT_BYTES),
        name="ffn_final" if final else "ffn",
    )(x2, g2, win, wout, gf)


def kernel(x, norm1_g, w_in, gm_ln_g, gm_ln_b, gm_ws, gm_bs, conv_w, conv_b, conv_ln_g, conv_ln_b,
           w_out, norm2_g, w_ffn_in, w_ffn_out, final_g):
    batch, seq, _ = x.shape
    depth = w_in.shape[0]
    assert seq % (STATE_CHUNKS * CHUNK) == 0 and seq % TOKEN_TILE == 0

    dmask, zf, zb, xif, xib, gc_f, gc_b = _retention_tables()
    cos2, sin2 = _rotary_tables(seq)
    row = lambda a: a.reshape(1, -1)

    w_in_b, w_out_b = w_in.astype(BF16), w_out.astype(BF16)
    w_ffn_in_b, w_ffn_out_b = w_ffn_in.astype(BF16), w_ffn_out.astype(BF16)

    x2 = x.reshape(batch * seq, D_MODEL)
    for l in range(depth):
        ws = gm_ws[l].reshape(GM_HEADS * CHUNK, CHUNK).astype(BF16)
        bs = jnp.repeat(gm_bs[l].T, GM_HEAD_DIM, axis=1)
        ygm, q, k, v, sg, hglu = _in_proj(
            x2, row(norm1_g[l]), w_in_b, l, cos2, sin2,
            row(gm_ln_g[l]), row(gm_ln_b[l]), ws, bs, seq)
        sf, sb = _ret_state(k, v, zf, zb, gc_f, gc_b, batch, seq)
        x2 = _mix_out(x2, ygm, q, k, v, sg, hglu, sf, sb, dmask, xif, xib,
                      conv_w[l], row(conv_b[l]), row(conv_ln_g[l]), row(conv_ln_b[l]),
                      w_out_b, l, seq)
        x2 = _ffn(x2, row(norm2_g[l]), w_ffn_in_b, w_ffn_out_b, l,
                  row(final_g), final=(l == depth - 1))
    return x2.reshape(batch, seq, D_MODEL)
```

```python
import functools

import numpy as np
import jax
import jax.numpy as jnp
from jax import lax
from jax.experimental import pallas as pl
from jax.experimental.pallas import tpu as pltpu

F32 = jnp.float32
BF16 = jnp.bfloat16

D_MODEL = 1024
GM_WIDTH = 256
GM_HEADS = 4
GM_HEAD_DIM = GM_WIDTH // GM_HEADS
CHUNK = 128
RET_WIDTH = 512
RET_HEADS = 4
RET_HEAD_DIM = RET_WIDTH // RET_HEADS
CONV_WIDTH = 256
CONV_KERNEL = 31
CONV_PAD = CONV_KERNEL // 2
IN_WIDTH = 2 * GM_WIDTH + 4 * RET_WIDTH + 2 * CONV_WIDTH
FFN_HIDDEN = 2816
ROPE_BASE = 10000.0
EPS = 1e-6

COL_GM = 0
COL_Q = 2 * GM_WIDTH
COL_V = COL_Q + 2 * RET_WIDTH
COL_CONV = COL_V + 2 * RET_WIDTH

TOKEN_TILE = 512
PROJ_TILE = 1024
FFN_TILE = 1024
HALO_ROWS = 16
STATE_CHUNKS = 16
CONV_ROW_BLOCK = 64
SUBLANES = 8
FFN_COL_BLOCK = 256
OUT_COL_BLOCK = 256
VMEM_LIMIT_BYTES = 56 * 1024 * 1024


def _gelu(x):
    return 0.5 * x * (1.0 + lax.erf(x * np.float32(1.0 / np.sqrt(2.0))))


def _sigmoid(x):
    return 1.0 / (1.0 + jnp.exp(-x))


def _standardize(x):
    mu = jnp.mean(x, axis=-1, keepdims=True)
    xc = x - mu
    var = jnp.mean(xc * xc, axis=-1, keepdims=True)
    return xc * lax.rsqrt(var + EPS)


def _rmsnorm(x, g):
    return x * lax.rsqrt(jnp.mean(x * x, axis=-1, keepdims=True) + EPS) * g


def _retention_tables():
    idx = np.arange(CHUNK, dtype=np.float32)
    gamma_f = (1.0 - np.exp2(-5.0 - np.arange(RET_HEADS, dtype=np.float32))).astype(np.float32)
    gamma_b = gamma_f[::-1]
    lf = np.log(gamma_f)[:, None]
    lb = np.log(gamma_b)[:, None]
    diff = idx[:, None] - idx[None, :]
    dmask = np.where(diff >= 0,
                     np.exp(lf[:, :, None] * np.maximum(diff, 0.0)),
                     np.exp(lb[:, :, None] * np.maximum(-diff, 0.0))).astype(np.float32)
    zeta_f = np.exp(lf * (CHUNK - 1 - idx))
    zeta_b = np.exp(lb * idx)
    xi_f = np.exp(lf * (idx + 1))
    xi_b = np.exp(lb * (CHUNK - idx))

    def per_row(t):
        return np.repeat(t.T.astype(np.float32), RET_HEAD_DIM, axis=1)

    gc_f = [float(np.exp(np.float32(l) * np.float32(CHUNK))) for l in lf[:, 0]]
    gc_b = [float(np.exp(np.float32(l) * np.float32(CHUNK))) for l in lb[:, 0]]
    return dmask, per_row(zeta_f), per_row(zeta_b), per_row(xi_f), per_row(xi_b), gc_f, gc_b


def _rotary_tables(seq):
    half = RET_HEAD_DIM // 2
    inv_freq = ROPE_BASE ** (-np.arange(half, dtype=np.float64) / half)
    ang = np.arange(seq, dtype=np.float64)[:, None] * inv_freq[None, :]
    cos, sin = np.cos(ang), np.sin(ang)
    cos2 = np.concatenate([cos, cos], axis=-1).astype(np.float32)
    sin2 = np.concatenate([-sin, sin], axis=-1).astype(np.float32)
    return cos2, sin2


def _layer_weight(layer, rows, cols, buffers=2):
    return pl.BlockSpec((None, rows, cols), lambda t: (layer, 0, 0),
                        pipeline_mode=pl.Buffered(buffers))


def _in_proj_body(x_ref, g1_ref, w_ref, cos_ref, sin_ref, lng_ref, lnb_ref, ws_ref, bs_ref,
                  ygm_ref, q_ref, k_ref, v_ref, sg_ref, hglu_ref):
    h = _rmsnorm(x_ref[...], g1_ref[...]).astype(BF16)

    def proj(lo, hi):
        return jnp.dot(h, w_ref[:, lo:hi], preferred_element_type=F32)

    uv = proj(COL_GM, COL_Q)
    u = _gelu(uv[:, :GM_WIDTH])
    v = _gelu(uv[:, GM_WIDTH:])
    vln = (_standardize(v) * lng_ref[...] + lnb_ref[...]).astype(BF16)
    lane_head = lax.broadcasted_iota(jnp.int32, (CHUNK, GM_WIDTH), 1) // GM_HEAD_DIM
    for c in range(PROJ_TILE // CHUNK):
        rows = slice(c * CHUNK, (c + 1) * CHUNK)
        full = jnp.dot(ws_ref[...], vln[rows], preferred_element_type=F32)
        mixed = full[0:CHUNK]
        for hd in range(1, GM_HEADS):
            mixed = jnp.where(lane_head == hd, full[hd * CHUNK:(hd + 1) * CHUNK], mixed)
        ygm_ref[rows, :] = (u[rows] * (mixed + bs_ref[...])).astype(BF16)

    cos2 = cos_ref[...]
    sin2 = sin_ref[...]
    qk = proj(COL_Q, COL_V)
    scale = np.float32(RET_HEAD_DIM ** -0.5)
    for hd in range(RET_HEADS):
        cols = slice(hd * RET_HEAD_DIM, (hd + 1) * RET_HEAD_DIM)
        t = qk[:, cols]
        q_ref[:, cols] = (t * cos2 + pltpu.roll(t, RET_HEAD_DIM // 2, 1) * sin2).astype(BF16)
        t = qk[:, RET_WIDTH + hd * RET_HEAD_DIM:RET_WIDTH + (hd + 1) * RET_HEAD_DIM]
        k_ref[:, cols] = ((t * cos2 + pltpu.roll(t, RET_HEAD_DIM // 2, 1) * sin2) * scale).astype(BF16)
    vg = proj(COL_V, COL_CONV)
    v_ref[...] = vg[:, :RET_WIDTH].astype(BF16)
    g = vg[:, RET_WIDTH:]
    sg_ref[...] = (g * _sigmoid(g)).astype(BF16)

    ag = proj(COL_CONV, IN_WIDTH)
    hglu_ref[...] = ag[:, :CONV_WIDTH] * _sigmoid(ag[:, CONV_WIDTH:])


def _in_proj(x2, g1, w_in, layer, cos2, sin2, lng, lnb, ws, bs, seq):
    m = x2.shape[0]
    tiles_per_seq = seq // PROJ_TILE
    tok = lambda width: pl.BlockSpec((PROJ_TILE, width), lambda t: (t, 0))
    const = lambda shape: pl.BlockSpec(shape, lambda t: (0,) * len(shape))
    return pl.pallas_call(
        _in_proj_body,
        grid=(m // PROJ_TILE,),
        in_specs=[
            tok(D_MODEL), const((1, D_MODEL)), _layer_weight(layer, D_MODEL, IN_WIDTH),
            pl.BlockSpec((PROJ_TILE, RET_HEAD_DIM), lambda t: (t % tiles_per_seq, 0)),
            pl.BlockSpec((PROJ_TILE, RET_HEAD_DIM), lambda t: (t % tiles_per_seq, 0)),
            const((1, GM_WIDTH)), const((1, GM_WIDTH)),
            const((GM_HEADS * CHUNK, CHUNK)), const((CHUNK, GM_WIDTH)),
        ],
        out_specs=[tok(GM_WIDTH), tok(RET_WIDTH), tok(RET_WIDTH), tok(RET_WIDTH), tok(RET_WIDTH),
                   tok(CONV_WIDTH)],
        out_shape=[
            jax.ShapeDtypeStruct((m, GM_WIDTH), BF16),
            jax.ShapeDtypeStruct((m, RET_WIDTH), BF16),
            jax.ShapeDtypeStruct((m, RET_WIDTH), BF16),
            jax.ShapeDtypeStruct((m, RET_WIDTH), BF16),
            jax.ShapeDtypeStruct((m, RET_WIDTH), BF16),
            jax.ShapeDtypeStruct((m, CONV_WIDTH), F32),
        ],
        compiler_params=pltpu.CompilerParams(
            dimension_semantics=("parallel",), vmem_limit_bytes=VMEM_LIMIT_BYTES),
        name="in_proj",
    )(x2, g1, w_in, cos2, sin2, lng, lnb, ws, bs)


def _ret_state_body(gc_f, gc_b, kf_ref, vf_ref, kb_ref, vb_ref, zf_ref, zb_ref,
                    sf_ref, sb_ref, st_ref):
    @pl.when(pl.program_id(1) == 0)
    def _():
        st_ref[...] = jnp.zeros_like(st_ref)

    def chunk_kv(k_ref, v_ref, z_ref, j):
        rows = slice(j * CHUNK, (j + 1) * CHUNK)
        vz = (v_ref[rows, :].astype(F32) * z_ref[...]).astype(BF16)
        out = []
        for hd in range(RET_HEADS):
            cols = slice(hd * RET_HEAD_DIM, (hd + 1) * RET_HEAD_DIM)
            out.append(lax.dot_general(k_ref[rows, cols], vz[:, cols], (((0,), (0,)), ((), ())),
                                       preferred_element_type=F32))
        return out

    for d, (k_ref, v_ref, z_ref, s_ref, gc, order) in enumerate((
            (kf_ref, vf_ref, zf_ref, sf_ref, gc_f, range(STATE_CHUNKS)),
            (kb_ref, vb_ref, zb_ref, sb_ref, gc_b, range(STATE_CHUNKS - 1, -1, -1)))):
        for j in order:
            kv = chunk_kv(k_ref, v_ref, z_ref, j)
            for hd in range(RET_HEADS):
                state = st_ref[d, hd]
                s_ref[j, hd] = state.astype(BF16)
                st_ref[d, hd] = gc[hd] * state + kv[hd]


def _ret_state(k, v, zf, zb, gc_f, gc_b, batch, seq):
    rows = STATE_CHUNKS * CHUNK
    nb = seq // rows
    n_chunks = batch * seq // CHUNK
    fwd = pl.BlockSpec((rows, RET_WIDTH), lambda b, i: (b * nb + i, 0))
    bwd = pl.BlockSpec((rows, RET_WIDTH), lambda b, i: (b * nb + nb - 1 - i, 0))
    const = pl.BlockSpec((CHUNK, RET_WIDTH), lambda b, i: (0, 0))
    state_shape = (STATE_CHUNKS, RET_HEADS, RET_HEAD_DIM, RET_HEAD_DIM)
    return pl.pallas_call(
        functools.partial(_ret_state_body, gc_f, gc_b),
        grid=(batch, nb),
        in_specs=[fwd, fwd, bwd, bwd, const, const],
        out_specs=[
            pl.BlockSpec(state_shape, lambda b, i: (b * nb + i, 0, 0, 0)),
            pl.BlockSpec(state_shape, lambda b, i: (b * nb + nb - 1 - i, 0, 0, 0)),
        ],
        out_shape=[jax.ShapeDtypeStruct((n_chunks, RET_HEADS, RET_HEAD_DIM, RET_HEAD_DIM), BF16)] * 2,
        scratch_shapes=[pltpu.VMEM((2, RET_HEADS, RET_HEAD_DIM, RET_HEAD_DIM), F32)],
        compiler_params=pltpu.CompilerParams(
            dimension_semantics=("parallel", "arbitrary"), vmem_limit_bytes=VMEM_LIMIT_BYTES),
        name="ret_state",
    )(k, v, k, v, zf, zb)


def _mix_out_body(tiles_per_seq, x_ref, ygm_ref, q_ref, k_ref, v_ref, sg_ref,
                  hmain_ref, hprev_ref, hnext_ref, sf_ref, sb_ref, dmask_ref, xif_ref, xib_ref,
                  cw_ref, cb_ref, clg_ref, clb_ref, wout_ref, o_ref,
                  hpad_ref, p_ref, oret_ref, yret_ref, ycv_ref):
    t = pl.program_id(0)
    first = (t % tiles_per_seq) == 0
    last = (t % tiles_per_seq) == tiles_per_seq - 1

    pairs = [(c, hd) for c in range(TOKEN_TILE // CHUNK) for hd in range(RET_HEADS)]

    def block(c, hd):
        return slice(c * CHUNK, (c + 1) * CHUNK), slice(hd * RET_HEAD_DIM, (hd + 1) * RET_HEAD_DIM)

    def masked_scores(i):
        c, hd = pairs[i]
        rows, cols = block(c, hd)
        scores = lax.dot_general(q_ref[rows, cols], k_ref[rows, cols], (((1,), (1,)), ((), ())),
                                 preferred_element_type=F32)
        p_ref[i] = (scores * dmask_ref[hd]).astype(BF16)

    def weighted_values(i):
        c, hd = pairs[i]
        rows, cols = block(c, hd)
        qh = q_ref[rows, cols]
        o = jnp.dot(p_ref[i], v_ref[rows, cols], preferred_element_type=F32)
        o = o + xif_ref[:, cols] * jnp.dot(qh, sf_ref[c, hd], preferred_element_type=F32)
        o = o + xib_ref[:, cols] * jnp.dot(qh, sb_ref[c, hd], preferred_element_type=F32)
        oret_ref[rows, cols] = o

    def head_norm_gate(hd):
        cols = slice(hd * RET_HEAD_DIM, (hd + 1) * RET_HEAD_DIM)
        yret_ref[:, cols] = (_standardize(oret_ref[:, cols])
                             * sg_ref[:, cols].astype(F32)).astype(BF16)

    def conv(rb):
        span = CONV_ROW_BLOCK + SUBLANES
        r0 = rb * CONV_ROW_BLOCK
        acc = None
        for r in range(SUBLANES):
            part = None
            for j in range(CONV_KERNEL):
                if (j - CONV_PAD) % SUBLANES != r:
                    continue
                start = r0 + HALO_ROWS + (j - CONV_PAD - r)
                term = cw_ref[j:j + 1, :] * hpad_ref[start:start + span, :]
                part = term if part is None else part + term
            if r:
                part = pltpu.roll(part, span - r, 0)
            part = part[:CONV_ROW_BLOCK]
            acc = part if acc is None else acc + part
        hc = _standardize(acc + cb_ref[...]) * clg_ref[...] + clb_ref[...]
        ycv_ref[r0:r0 + CONV_ROW_BLOCK, :] = (hc * _sigmoid(hc)).astype(BF16)

    zero_halo = jnp.zeros((HALO_ROWS, CONV_WIDTH), F32)
    hpad_ref[0:HALO_ROWS, :] = jnp.where(first, zero_halo, hprev_ref[...])
    hpad_ref[HALO_ROWS:HALO_ROWS + TOKEN_TILE, :] = hmain_ref[...]
    hpad_ref[HALO_ROWS + TOKEN_TILE:, :] = jnp.where(last, zero_halo, hnext_ref[...])

    def project_gm():
        o_ref[...] = x_ref[...] + jnp.dot(ygm_ref[...], wout_ref[0:GM_WIDTH, :],
                                          preferred_element_type=F32)

    def project_ret(n):
        cols = slice(n * OUT_COL_BLOCK, (n + 1) * OUT_COL_BLOCK)
        o_ref[:, cols] += jnp.dot(yret_ref[...], wout_ref[GM_WIDTH:GM_WIDTH + RET_WIDTH, cols],
                                  preferred_element_type=F32)

    project_gm()
    for i in range(len(pairs)):
        masked_scores(i)
    for i in range(len(pairs)):
        weighted_values(i)
    for hd in range(RET_HEADS):
        head_norm_gate(hd)
    for rb in range(TOKEN_TILE // CONV_ROW_BLOCK):
        conv(rb)
    for n in range(D_MODEL // OUT_COL_BLOCK):
        project_ret(n)
    o_ref[...] +=jnp.dot(ycv_ref[...], wout_ref[GM_WIDTH + RET_WIDTH:, :],
                          preferred_element_type=F32)


def _mix_out(x2, ygm, q, k, v, sg, hglu, sf, sb, dmask, xif, xib, cw, cb, clg, clb, wout, layer, seq):
    m = x2.shape[0]
    tiles_per_seq = seq // TOKEN_TILE
    halo_per_tile = TOKEN_TILE // HALO_ROWS
    n_halo = m // HALO_ROWS
    tok = lambda width: pl.BlockSpec((TOKEN_TILE, width), lambda t: (t, 0))
    const = lambda shape: pl.BlockSpec(shape, lambda t: (0,) * len(shape))
    state = pl.BlockSpec((TOKEN_TILE // CHUNK, RET_HEADS, RET_HEAD_DIM, RET_HEAD_DIM),
                         lambda t: (t, 0, 0, 0))
    return pl.pallas_call(
        functools.partial(_mix_out_body, tiles_per_seq),
        grid=(m // TOKEN_TILE,),
        in_specs=[
            tok(D_MODEL), tok(GM_WIDTH), tok(RET_WIDTH), tok(RET_WIDTH), tok(RET_WIDTH), tok(RET_WIDTH),
            tok(CONV_WIDTH),
            pl.BlockSpec((HALO_ROWS, CONV_WIDTH), lambda t: (jnp.maximum(t * halo_per_tile - 1, 0), 0)),
            pl.BlockSpec((HALO_ROWS, CONV_WIDTH),
                         lambda t: (jnp.minimum((t + 1) * halo_per_tile, n_halo - 1), 0)),
            state, state,
            const((RET_HEADS, CHUNK, CHUNK)), const((CHUNK, RET_WIDTH)), const((CHUNK, RET_WIDTH)),
            const((CONV_KERNEL, CONV_WIDTH)), const((1, CONV_WIDTH)), const((1, CONV_WIDTH)),
            const((1, CONV_WIDTH)), _layer_weight(layer, D_MODEL, D_MODEL),
        ],
        out_specs=tok(D_MODEL),
        out_shape=jax.ShapeDtypeStruct((m, D_MODEL), F32),
        scratch_shapes=[pltpu.VMEM((TOKEN_TILE + 2 * HALO_ROWS, CONV_WIDTH), F32),
                        pltpu.VMEM((TOKEN_TILE // CHUNK * RET_HEADS, CHUNK, CHUNK), BF16),
                        pltpu.VMEM((TOKEN_TILE, RET_WIDTH), F32),
                        pltpu.VMEM((TOKEN_TILE, RET_WIDTH), BF16),
                        pltpu.VMEM((TOKEN_TILE, CONV_WIDTH), BF16)],
        compiler_params=pltpu.CompilerParams(
            dimension_semantics=("parallel",), vmem_limit_bytes=VMEM_LIMIT_BYTES),
        name="mix_out",
    )(x2, ygm, q, k, v, sg, hglu, hglu, hglu, sf, sb, dmask, xif, xib, cw, cb, clg, clb, wout)


def _ffn_body(final, x_ref, g2_ref, win_ref, wout_ref, gf_ref, o_ref):
    x = x_ref[...]
    h = _rmsnorm(x, g2_ref[...]).astype(BF16)
    acc = x
    for j in range(FFN_HIDDEN // FFN_COL_BLOCK):
        lo = j * FFN_COL_BLOCK
        gate = jnp.dot(h, win_ref[:, lo:lo + FFN_COL_BLOCK], preferred_element_type=F32)
        up = jnp.dot(h, win_ref[:, FFN_HIDDEN + lo:FFN_HIDDEN + lo + FFN_COL_BLOCK],
                     preferred_element_type=F32)
        act = (gate * _sigmoid(gate) * up).astype(BF16)
        acc = acc + jnp.dot(act, wout_ref[lo:lo + FFN_COL_BLOCK, :], preferred_element_type=F32)
    o_ref[...] = _rmsnorm(acc, gf_ref[...]) if final else acc


def _ffn(x2, g2, win, wout, layer, gf, final):
    m = x2.shape[0]
    tok = pl.BlockSpec((FFN_TILE, D_MODEL), lambda t: (t, 0))
    const = lambda shape: pl.BlockSpec(shape, lambda t: (0,) * len(shape))
    return pl.pallas_call(
        functools.partial(_ffn_body, final),
        grid=(m // FFN_TILE,),
        in_specs=[tok, const((1, D_MODEL)), _layer_weight(layer, D_MODEL, 2 * FFN_HIDDEN, 1),
                  _layer_weight(layer, FFN_HIDDEN, D_MODEL, 1), const((1, D_MODEL))],
        out_specs=tok,
        out_shape=jax.ShapeDtypeStruct((m, D_MODEL), F32),
        compiler_params=pltpu.CompilerParams(
            dimension_semantics=("parallel",), vmem_limit_bytes=VMEM_LIMIT_BYTES),
        name="ffn_final" if final else "ffn",
    )(x2, g2, win, wout, gf)


def kernel(x, norm1_g, w_in, gm_ln_g, gm_ln_b, gm_ws, gm_bs, conv_w, conv_b, conv_ln_g, conv_ln_b,
           w_out, norm2_g, w_ffn_in, w_ffn_out, final_g):
    batch, seq, _ = x.shape
    depth = w_in.shape[0]
    assert all(seq % t == 0 for t in (STATE_CHUNKS * CHUNK, TOKEN_TILE, PROJ_TILE, FFN_TILE))

    dmask, zf, zb, xif, xib, gc_f, gc_b = _retention_tables()
    cos2, sin2 = _rotary_tables(seq)
    row = lambda a: a.reshape(1, -1)

    w_in_b, w_out_b = w_in.astype(BF16), w_out.astype(BF16)
    w_ffn_in_b, w_ffn_out_b = w_ffn_in.astype(BF16), w_ffn_out.astype(BF16)

    x2 = x.reshape(batch * seq, D_MODEL)
    for l in range(depth):
        ws = gm_ws[l].reshape(GM_HEADS * CHUNK, CHUNK).astype(BF16)
        bs = jnp.repeat(gm_bs[l].T, GM_HEAD_DIM, axis=1)
        ygm, q, k, v, sg, hglu = _in_proj(
            x2, row(norm1_g[l]), w_in_b, l, cos2, sin2,
            row(gm_ln_g[l]), row(gm_ln_b[l]), ws, bs, seq)
        sf, sb = _ret_state(k, v, zf, zb, gc_f, gc_b, batch, seq)
        x2 = _mix_out(x2, ygm, q, k, v, sg, hglu, sf, sb, dmask, xif, xib,
                      conv_w[l], row(conv_b[l]), row(conv_ln_g[l]), row(conv_ln_b[l]),
                      w_out_b, l, seq)
        x2 = _ffn(x2, row(norm2_g[l]), w_ffn_in_b, w_ffn_out_b, l,
                  row(final_g), final=(l == depth - 1))
    return x2.reshape(batch, seq, D_MODEL)
```

```python
import functools

import numpy as np
import jax
import jax.numpy as jnp
from jax import lax
from jax.experimental import pallas as pl
from jax.experimental.pallas import tpu as pltpu

F32 = jnp.float32
BF16 = jnp.bfloat16

D_MODEL = 1024
GM_WIDTH = 256
GM_HEADS = 4
GM_HEAD_DIM = GM_WIDTH // GM_HEADS
CHUNK = 128
RET_WIDTH = 512
RET_HEADS = 4
RET_HEAD_DIM = RET_WIDTH // RET_HEADS
CONV_WIDTH = 256
CONV_KERNEL = 31
CONV_PAD = CONV_KERNEL // 2
IN_WIDTH = 2 * GM_WIDTH + 4 * RET_WIDTH + 2 * CONV_WIDTH
FFN_HIDDEN = 2816
ROPE_BASE = 10000.0
EPS = 1e-6

COL_GM = 0
COL_Q = 2 * GM_WIDTH
COL_V = COL_Q + 2 * RET_WIDTH
COL_CONV = COL_V + 2 * RET_WIDTH

TOKEN_TILE = 512
PROJ_TILE = 1024
FFN_TILE = 1024
HALO_ROWS = 16
STATE_CHUNKS = 16
CONV_ROW_BLOCK = 64
SUBLANES = 8
FFN_COL_BLOCK = 256
OUT_COL_BLOCK = 256
VMEM_LIMIT_BYTES = 56 * 1024 * 1024


def _gelu(x):
    return 0.5 * x * (1.0 + lax.erf(x * np.float32(1.0 / np.sqrt(2.0))))


def _sigmoid(x):
    return 1.0 / (1.0 + jnp.exp(-x))


def _standardize(x):
    mu = jnp.mean(x, axis=-1, keepdims=True)
    xc = x - mu
    var = jnp.mean(xc * xc, axis=-1, keepdims=True)
    return xc * lax.rsqrt(var + EPS)


def _rmsnorm(x, g):
    return x * lax.rsqrt(jnp.mean(x * x, axis=-1, keepdims=True) + EPS) * g


def _retention_tables():
    idx = np.arange(CHUNK, dtype=np.float32)
    gamma_f = (1.0 - np.exp2(-5.0 - np.arange(RET_HEADS, dtype=np.float32))).astype(np.float32)
    gamma_b = gamma_f[::-1]
    lf = np.log(gamma_f)[:, None]
    lb = np.log(gamma_b)[:, None]
    diff = idx[:, None] - idx[None, :]
    dmask = np.where(diff >= 0,
                     np.exp(lf[:, :, None] * np.maximum(diff, 0.0)),
                     np.exp(lb[:, :, None] * np.maximum(-diff, 0.0))).astype(np.float32)
    zeta_f = np.exp(lf * (CHUNK - 1 - idx))
    zeta_b = np.exp(lb * idx)
    xi_f = np.exp(lf * (idx + 1))
    xi_b = np.exp(lb * (CHUNK - idx))

    def per_row(t):
        return np.repeat(t.T.astype(np.float32), RET_HEAD_DIM, axis=1)

    gc_f = [float(np.exp(np.float32(l) * np.float32(CHUNK))) for l in lf[:, 0]]
    gc_b = [float(np.exp(np.float32(l) * np.float32(CHUNK))) for l in lb[:, 0]]
    return dmask, per_row(zeta_f), per_row(zeta_b), per_row(xi_f), per_row(xi_b), gc_f, gc_b


def _rotary_tables(seq):
    half = RET_HEAD_DIM // 2
    inv_freq = ROPE_BASE ** (-np.arange(half, dtype=np.float64) / half)
    ang = np.arange(seq, dtype=np.float64)[:, None] * inv_freq[None, :]
    cos, sin = np.cos(ang), np.sin(ang)
    cos2 = np.concatenate([cos, cos], axis=-1).astype(np.float32)
    sin2 = np.concatenate([-sin, sin], axis=-1).astype(np.float32)
    return cos2, sin2


def _layer_weight(layer, rows, cols, buffers=2):
    return pl.BlockSpec((None, rows, cols), lambda t: (layer, 0, 0),
                        pipeline_mode=pl.Buffered(buffers))


def _in_proj_body(x_ref, g1_ref, w_ref, cos_ref, sin_ref, lng_ref, lnb_ref, ws_ref, bs_ref,
                  ygm_ref, q_ref, k_ref, v_ref, sg_ref, hglu_ref):
    h = _rmsnorm(x_ref[...], g1_ref[...]).astype(BF16)

    def proj(lo, hi):
        return jnp.dot(h, w_ref[:, lo:hi], preferred_element_type=F32)

    uv = proj(COL_GM, COL_Q)
    u = _gelu(uv[:, :GM_WIDTH])
    v = _gelu(uv[:, GM_WIDTH:])
    vln = (_standardize(v) * lng_ref[...] + lnb_ref[...]).astype(BF16)
    lane_head = lax.broadcasted_iota(jnp.int32, (CHUNK, GM_WIDTH), 1) // GM_HEAD_DIM
    for c in range(PROJ_TILE // CHUNK):
        rows = slice(c * CHUNK, (c + 1) * CHUNK)
        full = jnp.dot(ws_ref[...], vln[rows], preferred_element_type=F32)
        mixed = full[0:CHUNK]
        for hd in range(1, GM_HEADS):
            mixed = jnp.where(lane_head == hd, full[hd * CHUNK:(hd + 1) * CHUNK], mixed)
        ygm_ref[rows, :] = (u[rows] * (mixed + bs_ref[...])).astype(BF16)

    cos2 = cos_ref[...]
    sin2 = sin_ref[...]
    qk = proj(COL_Q, COL_V)
    scale = np.float32(RET_HEAD_DIM ** -0.5)
    for hd in range(RET_HEADS):
        cols = slice(hd * RET_HEAD_DIM, (hd + 1) * RET_HEAD_DIM)
        t = qk[:, cols]
        q_ref[:, cols] = (t * cos2 + pltpu.roll(t, RET_HEAD_DIM // 2, 1) * sin2).astype(BF16)
        t = qk[:, RET_WIDTH + hd * RET_HEAD_DIM:RET_WIDTH + (hd + 1) * RET_HEAD_DIM]
        k_ref[:, cols] = ((t * cos2 + pltpu.roll(t, RET_HEAD_DIM // 2, 1) * sin2) * scale).astype(BF16)
    vg = proj(COL_V, COL_CONV)
    v_ref[...] = vg[:, :RET_WIDTH].astype(BF16)
    g = vg[:, RET_WIDTH:]
    sg_ref[...] = (g * _sigmoid(g)).astype(BF16)

    ag = proj(COL_CONV, IN_WIDTH)
    hglu_ref[...] = ag[:, :CONV_WIDTH] * _sigmoid(ag[:, CONV_WIDTH:])


def _in_proj(x2, g1, w_in, layer, cos2, sin2, lng, lnb, ws, bs, seq):
    m = x2.shape[0]
    tiles_per_seq = seq // PROJ_TILE
    tok = lambda width: pl.BlockSpec((PROJ_TILE, width), lambda t: (t, 0))
    const = lambda shape: pl.BlockSpec(shape, lambda t: (0,) * len(shape))
    return pl.pallas_call(
        _in_proj_body,
        grid=(m // PROJ_TILE,),
        in_specs=[
            tok(D_MODEL), const((1, D_MODEL)), _layer_weight(layer, D_MODEL, IN_WIDTH),
            pl.BlockSpec((PROJ_TILE, RET_HEAD_DIM), lambda t: (t % tiles_per_seq, 0)),
            pl.BlockSpec((PROJ_TILE, RET_HEAD_DIM), lambda t: (t % tiles_per_seq, 0)),
            const((1, GM_WIDTH)), const((1, GM_WIDTH)),
            const((GM_HEADS * CHUNK, CHUNK)), const((CHUNK, GM_WIDTH)),
        ],
        out_specs=[tok(GM_WIDTH), tok(RET_WIDTH), tok(RET_WIDTH), tok(RET_WIDTH), tok(RET_WIDTH),
                   tok(CONV_WIDTH)],
        out_shape=[
            jax.ShapeDtypeStruct((m, GM_WIDTH), BF16),
            jax.ShapeDtypeStruct((m, RET_WIDTH), BF16),
            jax.ShapeDtypeStruct((m, RET_WIDTH), BF16),
            jax.ShapeDtypeStruct((m, RET_WIDTH), BF16),
            jax.ShapeDtypeStruct((m, RET_WIDTH), BF16),
            jax.ShapeDtypeStruct((m, CONV_WIDTH), F32),
        ],
        compiler_params=pltpu.CompilerParams(
            dimension_semantics=("parallel",), vmem_limit_bytes=VMEM_LIMIT_BYTES),
        name="in_proj",
    )(x2, g1, w_in, cos2, sin2, lng, lnb, ws, bs)


def _ret_state_body(gc_f, gc_b, n_blocks, kf_ref, vf_ref, kb_ref, vb_ref, zf_ref, zb_ref,
                    hmain_ref, hprev_ref, hnext_ref, cw_ref, cb_ref, clg_ref, clb_ref,
                    sf_ref, sb_ref, ycv_ref, st_ref, hpad_ref):
    i = pl.program_id(1)

    @pl.when(i == 0)
    def _():
        st_ref[...] = jnp.zeros_like(st_ref)

    rows_per_step = STATE_CHUNKS * CHUNK
    zero_halo = jnp.zeros((HALO_ROWS, CONV_WIDTH), F32)
    hpad_ref[0:HALO_ROWS, :] = jnp.where(i == 0, zero_halo, hprev_ref[...])
    hpad_ref[HALO_ROWS:HALO_ROWS + rows_per_step, :] = hmain_ref[...]
    hpad_ref[HALO_ROWS + rows_per_step:, :] = jnp.where(i == n_blocks - 1, zero_halo, hnext_ref[...])
    span = CONV_ROW_BLOCK + SUBLANES
    for rb in range(rows_per_step // CONV_ROW_BLOCK):
        r0 = rb * CONV_ROW_BLOCK
        acc = None
        for r in range(SUBLANES):
            part = None
            for j in range(CONV_KERNEL):
                if (j - CONV_PAD) % SUBLANES != r:
                    continue
                start = r0 + HALO_ROWS + (j - CONV_PAD - r)
                term = cw_ref[j:j + 1, :] * hpad_ref[start:start + span, :]
                part = term if part is None else part + term
            if r:
                part = pltpu.roll(part, span - r, 0)
            part = part[:CONV_ROW_BLOCK]
            acc = part if acc is None else acc + part
        hc = _standardize(acc + cb_ref[...]) * clg_ref[...] + clb_ref[...]
        ycv_ref[r0:r0 + CONV_ROW_BLOCK, :] = (hc * _sigmoid(hc)).astype(BF16)

    def chunk_kv(k_ref, v_ref, z_ref, j):
        rows = slice(j * CHUNK, (j + 1) * CHUNK)
        vz = (v_ref[rows, :].astype(F32) * z_ref[...]).astype(BF16)
        out = []
        for hd in range(RET_HEADS):
            cols = slice(hd * RET_HEAD_DIM, (hd + 1) * RET_HEAD_DIM)
            out.append(lax.dot_general(k_ref[rows, cols], vz[:, cols], (((0,), (0,)), ((), ())),
                                       preferred_element_type=F32))
        return out

    for d, (k_ref, v_ref, z_ref, s_ref, gc, order) in enumerate((
            (kf_ref, vf_ref, zf_ref, sf_ref, gc_f, range(STATE_CHUNKS)),
            (kb_ref, vb_ref, zb_ref, sb_ref, gc_b, range(STATE_CHUNKS - 1, -1, -1)))):
        for j in order:
            kv = chunk_kv(k_ref, v_ref, z_ref, j)
            for hd in range(RET_HEADS):
                state = st_ref[d, hd]
                s_ref[j, hd] = state.astype(BF16)
                st_ref[d, hd] = gc[hd] * state + kv[hd]


def _ret_state(k, v, zf, zb, gc_f, gc_b, hglu, cw, cb, clg, clb, batch, seq):
    rows = STATE_CHUNKS * CHUNK
    nb = seq // rows
    m = batch * seq
    n_chunks = m // CHUNK
    halo_per_step = rows // HALO_ROWS
    n_halo = m // HALO_ROWS
    step = lambda b, i: b * nb + i
    fwd = pl.BlockSpec((rows, RET_WIDTH), lambda b, i: (step(b, i), 0))
    bwd = pl.BlockSpec((rows, RET_WIDTH), lambda b, i: (b * nb + nb - 1 - i, 0))
    const = lambda shape: pl.BlockSpec(shape, lambda b, i: (0,) * len(shape))
    state_shape = (STATE_CHUNKS, RET_HEADS, RET_HEAD_DIM, RET_HEAD_DIM)
    return pl.pallas_call(
        functools.partial(_ret_state_body, gc_f, gc_b, nb),
        grid=(batch, nb),
        in_specs=[
            fwd, fwd, bwd, bwd, const((CHUNK, RET_WIDTH)), const((CHUNK, RET_WIDTH)),
            pl.BlockSpec((rows, CONV_WIDTH), lambda b, i: (step(b, i), 0)),
            pl.BlockSpec((HALO_ROWS, CONV_WIDTH),
                         lambda b, i: (jnp.maximum(step(b, i) * halo_per_step - 1, 0), 0)),
            pl.BlockSpec((HALO_ROWS, CONV_WIDTH),
                         lambda b, i: (jnp.minimum((step(b, i) + 1) * halo_per_step, n_halo - 1), 0)),
            const((CONV_KERNEL, CONV_WIDTH)), const((1, CONV_WIDTH)), const((1, CONV_WIDTH)),
            const((1, CONV_WIDTH)),
        ],
        out_specs=[
            pl.BlockSpec(state_shape, lambda b, i: (step(b, i), 0, 0, 0)),
            pl.BlockSpec(state_shape, lambda b, i: (b * nb + nb - 1 - i, 0, 0, 0)),
            pl.BlockSpec((rows, CONV_WIDTH), lambda b, i: (step(b, i), 0)),
        ],
        out_shape=[jax.ShapeDtypeStruct((n_chunks, RET_HEADS, RET_HEAD_DIM, RET_HEAD_DIM), BF16)] * 2
        + [jax.ShapeDtypeStruct((m, CONV_WIDTH), BF16)],
        scratch_shapes=[pltpu.VMEM((2, RET_HEADS, RET_HEAD_DIM, RET_HEAD_DIM), F32),
                        pltpu.VMEM((rows + 2 * HALO_ROWS, CONV_WIDTH), F32)],
        compiler_params=pltpu.CompilerParams(
            dimension_semantics=("parallel", "arbitrary"), vmem_limit_bytes=VMEM_LIMIT_BYTES),
        name="ret_state",
    )(k, v, k, v, zf, zb, hglu, hglu, hglu, cw, cb, clg, clb)


def _mix_out_body(x_ref, ygm_ref, q_ref, k_ref, v_ref, sg_ref, ycv_ref, sf_ref, sb_ref,
                  dmask_ref, xif_ref, xib_ref, wout_ref, o_ref, p_ref, oret_ref, yret_ref):
    pairs = [(c, hd) for c in range(TOKEN_TILE // CHUNK) for hd in range(RET_HEADS)]

    def block(c, hd):
        return slice(c * CHUNK, (c + 1) * CHUNK), slice(hd * RET_HEAD_DIM, (hd + 1) * RET_HEAD_DIM)

    def masked_scores(i):
        c, hd = pairs[i]
        rows, cols = block(c, hd)
        scores = lax.dot_general(q_ref[rows, cols], k_ref[rows, cols], (((1,), (1,)), ((), ())),
                                 preferred_element_type=F32)
        p_ref[i] = (scores * dmask_ref[hd]).astype(BF16)

    def weighted_values(i):
        c, hd = pairs[i]
        rows, cols = block(c, hd)
        qh = q_ref[rows, cols]
        o = jnp.dot(p_ref[i], v_ref[rows, cols], preferred_element_type=F32)
        o = o + xif_ref[:, cols] * jnp.dot(qh, sf_ref[c, hd], preferred_element_type=F32)
        o = o + xib_ref[:, cols] * jnp.dot(qh, sb_ref[c, hd], preferred_element_type=F32)
        oret_ref[rows, cols] = o

    def head_norm_gate(hd):
        cols = slice(hd * RET_HEAD_DIM, (hd + 1) * RET_HEAD_DIM)
        yret_ref[:, cols] = (_standardize(oret_ref[:, cols])
                             * sg_ref[:, cols].astype(F32)).astype(BF16)

    o_ref[...] = (x_ref[...]
                  + jnp.dot(ygm_ref[...], wout_ref[0:GM_WIDTH, :], preferred_element_type=F32)
                  + jnp.dot(ycv_ref[...], wout_ref[GM_WIDTH + RET_WIDTH:, :],
                            preferred_element_type=F32))
    for i in range(len(pairs)):
        masked_scores(i)
    for i in range(len(pairs)):
        weighted_values(i)
    for hd in range(RET_HEADS):
        head_norm_gate(hd)
    for n in range(D_MODEL // OUT_COL_BLOCK):
        cols = slice(n * OUT_COL_BLOCK, (n + 1) * OUT_COL_BLOCK)
        o_ref[:, cols] += jnp.dot(yret_ref[...], wout_ref[GM_WIDTH:GM_WIDTH + RET_WIDTH, cols],
                                  preferred_element_type=F32)


def _mix_out(x2, ygm, q, k, v, sg, ycv, sf, sb, dmask, xif, xib, wout, layer):
    m = x2.shape[0]
    tok = lambda width: pl.BlockSpec((TOKEN_TILE, width), lambda t: (t, 0))
    const = lambda shape: pl.BlockSpec(shape, lambda t: (0,) * len(shape))
    state = pl.BlockSpec((TOKEN_TILE // CHUNK, RET_HEADS, RET_HEAD_DIM, RET_HEAD_DIM),
                         lambda t: (t, 0, 0, 0))
    return pl.pallas_call(
        _mix_out_body,
        grid=(m // TOKEN_TILE,),
        in_specs=[
            tok(D_MODEL), tok(GM_WIDTH), tok(RET_WIDTH), tok(RET_WIDTH), tok(RET_WIDTH), tok(RET_WIDTH),
            tok(CONV_WIDTH), state, state,
            const((RET_HEADS, CHUNK, CHUNK)), const((CHUNK, RET_WIDTH)), const((CHUNK, RET_WIDTH)),
            _layer_weight(layer, D_MODEL, D_MODEL),
        ],
        out_specs=tok(D_MODEL),
        out_shape=jax.ShapeDtypeStruct((m, D_MODEL), F32),
        scratch_shapes=[pltpu.VMEM((TOKEN_TILE // CHUNK * RET_HEADS, CHUNK, CHUNK), BF16),
                        pltpu.VMEM((TOKEN_TILE, RET_WIDTH), F32),
                        pltpu.VMEM((TOKEN_TILE, RET_WIDTH), BF16)],
        compiler_params=pltpu.CompilerParams(
            dimension_semantics=("parallel",), vmem_limit_bytes=VMEM_LIMIT_BYTES),
        name="mix_out",
    )(x2, ygm, q, k, v, sg, ycv, sf, sb, dmask, xif, xib, wout)


def _ffn_body(final, x_ref, g2_ref, win_ref, wout_ref, gf_ref, o_ref):
    x = x_ref[...]
    h = _rmsnorm(x, g2_ref[...]).astype(BF16)
    acc = x
    for j in range(FFN_HIDDEN // FFN_COL_BLOCK):
        lo = j * FFN_COL_BLOCK
        gate = jnp.dot(h, win_ref[:, lo:lo + FFN_COL_BLOCK], preferred_element_type=F32)
        up = jnp.dot(h, win_ref[:, FFN_HIDDEN + lo:FFN_HIDDEN + lo + FFN_COL_BLOCK],
                     preferred_element_type=F32)
        act = (gate * _sigmoid(gate) * up).astype(BF16)
        acc = acc + jnp.dot(act, wout_ref[lo:lo + FFN_COL_BLOCK, :], preferred_element_type=F32)
    o_ref[...] = _rmsnorm(acc, gf_ref[...]) if final else acc


def _ffn(x2, g2, win, wout, layer, gf, final):
    m = x2.shape[0]
    tok = pl.BlockSpec((FFN_TILE, D_MODEL), lambda t: (t, 0))
    const = lambda shape: pl.BlockSpec(shape, lambda t: (0,) * len(shape))
    return pl.pallas_call(
        functools.partial(_ffn_body, final),
        grid=(m // FFN_TILE,),
        in_specs=[tok, const((1, D_MODEL)), _layer_weight(layer, D_MODEL, 2 * FFN_HIDDEN, 1),
                  _layer_weight(layer, FFN_HIDDEN, D_MODEL, 1), const((1, D_MODEL))],
        out_specs=tok,
        out_shape=jax.ShapeDtypeStruct((m, D_MODEL), F32),
        compiler_params=pltpu.CompilerParams(
            dimension_semantics=("parallel",), vmem_limit_bytes=VMEM_LIMIT_BYTES),
        name="ffn_final" if final else "ffn",
    )(x2, g2, win, wout, gf)


def kernel(x, norm1_g, w_in, gm_ln_g, gm_ln_b, gm_ws, gm_bs, conv_w, conv_b, conv_ln_g, conv_ln_b,
           w_out, norm2_g, w_ffn_in, w_ffn_out, final_g):
    batch, seq, _ = x.shape
    depth = w_in.shape[0]
    assert all(seq % t == 0 for t in (STATE_CHUNKS * CHUNK, TOKEN_TILE, PROJ_TILE, FFN_TILE))

    dmask, zf, zb, xif, xib, gc_f, gc_b = _retention_tables()
    cos2, sin2 = _rotary_tables(seq)
    row = lambda a: a.reshape(1, -1)

    w_in_b, w_out_b = w_in.astype(BF16), w_out.astype(BF16)
    w_ffn_in_b, w_ffn_out_b = w_ffn_in.astype(BF16), w_ffn_out.astype(BF16)

    x2 = x.reshape(batch * seq, D_MODEL)
    for l in range(depth):
        ws = gm_ws[l].reshape(GM_HEADS * CHUNK, CHUNK).astype(BF16)
        bs = jnp.repeat(gm_bs[l].T, GM_HEAD_DIM, axis=1)
        ygm, q, k, v, sg, hglu = _in_proj(
            x2, row(norm1_g[l]), w_in_b, l, cos2, sin2,
            row(gm_ln_g[l]), row(gm_ln_b[l]), ws, bs, seq)
        sf, sb, ycv = _ret_state(k, v, zf, zb, gc_f, gc_b, hglu, conv_w[l], row(conv_b[l]),
                                 row(conv_ln_g[l]), row(conv_ln_b[l]), batch, seq)
        x2 = _mix_out(x2, ygm, q, k, v, sg, ycv, sf, sb, dmask, xif, xib, w_out_b, l)
        x2 = _ffn(x2, row(norm2_g[l]), w_ffn_in_b, w_ffn_out_b, l,
                  row(final_g), final=(l == depth - 1))
    return x2.reshape(batch, seq, D_MODEL)
```

```python
import functools

import numpy as np
import jax
import jax.numpy as jnp
from jax import lax
from jax.experimental import pallas as pl
from jax.experimental.pallas import tpu as pltpu

F32 = jnp.float32
BF16 = jnp.bfloat16

D_MODEL = 1024
GM_WIDTH = 256
GM_HEADS = 4
GM_HEAD_DIM = GM_WIDTH // GM_HEADS
CHUNK = 128
RET_WIDTH = 512
RET_HEADS = 4
RET_HEAD_DIM = RET_WIDTH // RET_HEADS
CONV_WIDTH = 256
CONV_KERNEL = 31
CONV_PAD = CONV_KERNEL // 2
IN_WIDTH = 2 * GM_WIDTH + 4 * RET_WIDTH + 2 * CONV_WIDTH
FFN_HIDDEN = 2816
ROPE_BASE = 10000.0
EPS = 1e-6

COL_GM = 0
COL_Q = 2 * GM_WIDTH
COL_V = COL_Q + 2 * RET_WIDTH
COL_CONV = COL_V + 2 * RET_WIDTH

TOKEN_TILE = 512
PROJ_TILE = 1024
HALO_ROWS = 16
STATE_CHUNKS = 16
CONV_ROW_BLOCK = 64
SUBLANES = 8
FFN_COL_BLOCK = 256
OUT_COL_BLOCK = 256
VMEM_LIMIT_BYTES = 56 * 1024 * 1024


def _gelu(x):
    return 0.5 * x * (1.0 + lax.erf(x * np.float32(1.0 / np.sqrt(2.0))))


def _sigmoid(x):
    return 1.0 / (1.0 + jnp.exp(-x))


def _standardize(x):
    mu = jnp.mean(x, axis=-1, keepdims=True)
    xc = x - mu
    var = jnp.mean(xc * xc, axis=-1, keepdims=True)
    return xc * lax.rsqrt(var + EPS)


def _rmsnorm(x, g):
    return x * lax.rsqrt(jnp.mean(x * x, axis=-1, keepdims=True) + EPS) * g


def _retention_tables():
    idx = np.arange(CHUNK, dtype=np.float32)
    gamma_f = (1.0 - np.exp2(-5.0 - np.arange(RET_HEADS, dtype=np.float32))).astype(np.float32)
    gamma_b = gamma_f[::-1]
    lf = np.log(gamma_f)[:, None]
    lb = np.log(gamma_b)[:, None]
    diff = idx[:, None] - idx[None, :]
    dmask = np.where(diff >= 0,
                     np.exp(lf[:, :, None] * np.maximum(diff, 0.0)),
                     np.exp(lb[:, :, None] * np.maximum(-diff, 0.0))).astype(np.float32)
    zeta_f = np.exp(lf * (CHUNK - 1 - idx))
    zeta_b = np.exp(lb * idx)
    xi_f = np.exp(lf * (idx + 1))
    xi_b = np.exp(lb * (CHUNK - idx))

    def per_row(t):
        return np.repeat(t.T.astype(np.float32), RET_HEAD_DIM, axis=1)

    gc_f = [float(np.exp(np.float32(l) * np.float32(CHUNK))) for l in lf[:, 0]]
    gc_b = [float(np.exp(np.float32(l) * np.float32(CHUNK))) for l in lb[:, 0]]
    return dmask, per_row(zeta_f), per_row(zeta_b), per_row(xi_f), per_row(xi_b), gc_f, gc_b


def _rotary_tables(seq):
    half = RET_HEAD_DIM // 2
    inv_freq = ROPE_BASE ** (-np.arange(half, dtype=np.float64) / half)
    ang = np.arange(seq, dtype=np.float64)[:, None] * inv_freq[None, :]
    cos, sin = np.cos(ang), np.sin(ang)
    cos2 = np.concatenate([cos, cos], axis=-1).astype(np.float32)
    sin2 = np.concatenate([-sin, sin], axis=-1).astype(np.float32)
    return cos2, sin2


def _layer_weight(layer, rows, cols, buffers=2):
    return pl.BlockSpec((None, rows, cols), lambda t: (layer, 0, 0),
                        pipeline_mode=pl.Buffered(buffers))


def _in_proj_body(x_ref, g1_ref, w_ref, cos_ref, sin_ref, lng_ref, lnb_ref, ws_ref, bs_ref,
                  ygm_ref, q_ref, k_ref, v_ref, sg_ref, hglu_ref):
    h = _rmsnorm(x_ref[...], g1_ref[...]).astype(BF16)

    def proj(lo, hi):
        return jnp.dot(h, w_ref[:, lo:hi], preferred_element_type=F32)

    uv = proj(COL_GM, COL_Q)
    u = _gelu(uv[:, :GM_WIDTH])
    v = _gelu(uv[:, GM_WIDTH:])
    vln = (_standardize(v) * lng_ref[...] + lnb_ref[...]).astype(BF16)
    lane_head = lax.broadcasted_iota(jnp.int32, (CHUNK, GM_WIDTH), 1) // GM_HEAD_DIM
    for c in range(PROJ_TILE // CHUNK):
        rows = slice(c * CHUNK, (c + 1) * CHUNK)
        full = jnp.dot(ws_ref[...], vln[rows], preferred_element_type=F32)
        mixed = full[0:CHUNK]
        for hd in range(1, GM_HEADS):
            mixed = jnp.where(lane_head == hd, full[hd * CHUNK:(hd + 1) * CHUNK], mixed)
        ygm_ref[rows, :] = (u[rows] * (mixed + bs_ref[...])).astype(BF16)

    cos2 = cos_ref[...]
    sin2 = sin_ref[...]
    qk = proj(COL_Q, COL_V)
    scale = np.float32(RET_HEAD_DIM ** -0.5)
    for hd in range(RET_HEADS):
        cols = slice(hd * RET_HEAD_DIM, (hd + 1) * RET_HEAD_DIM)
        t = qk[:, cols]
        q_ref[:, cols] = (t * cos2 + pltpu.roll(t, RET_HEAD_DIM // 2, 1) * sin2).astype(BF16)
        t = qk[:, RET_WIDTH + hd * RET_HEAD_DIM:RET_WIDTH + (hd + 1) * RET_HEAD_DIM]
        k_ref[:, cols] = ((t * cos2 + pltpu.roll(t, RET_HEAD_DIM // 2, 1) * sin2) * scale).astype(BF16)
    vg = proj(COL_V, COL_CONV)
    v_ref[...] = vg[:, :RET_WIDTH].astype(BF16)
    g = vg[:, RET_WIDTH:]
    sg_ref[...] = (g * _sigmoid(g)).astype(BF16)

    ag = proj(COL_CONV, IN_WIDTH)
    hglu_ref[...] = ag[:, :CONV_WIDTH] * _sigmoid(ag[:, CONV_WIDTH:])


def _in_proj(x2, g1, w_in, layer, cos2, sin2, lng, lnb, ws, bs, seq):
    m = x2.shape[0]
    tiles_per_seq = seq // PROJ_TILE
    tok = lambda width: pl.BlockSpec((PROJ_TILE, width), lambda t: (t, 0))
    const = lambda shape: pl.BlockSpec(shape, lambda t: (0,) * len(shape))
    return pl.pallas_call(
        _in_proj_body,
        grid=(m // PROJ_TILE,),
        in_specs=[
            tok(D_MODEL), const((1, D_MODEL)), _layer_weight(layer, D_MODEL, IN_WIDTH),
            pl.BlockSpec((PROJ_TILE, RET_HEAD_DIM), lambda t: (t % tiles_per_seq, 0)),
            pl.BlockSpec((PROJ_TILE, RET_HEAD_DIM), lambda t: (t % tiles_per_seq, 0)),
            const((1, GM_WIDTH)), const((1, GM_WIDTH)),
            const((GM_HEADS * CHUNK, CHUNK)), const((CHUNK, GM_WIDTH)),
        ],
        out_specs=[tok(GM_WIDTH), tok(RET_WIDTH), tok(RET_WIDTH), tok(RET_WIDTH), tok(RET_WIDTH),
                   tok(CONV_WIDTH)],
        out_shape=[
            jax.ShapeDtypeStruct((m, GM_WIDTH), BF16),
            jax.ShapeDtypeStruct((m, RET_WIDTH), BF16),
            jax.ShapeDtypeStruct((m, RET_WIDTH), BF16),
            jax.ShapeDtypeStruct((m, RET_WIDTH), BF16),
            jax.ShapeDtypeStruct((m, RET_WIDTH), BF16),
            jax.ShapeDtypeStruct((m, CONV_WIDTH), F32),
        ],
        compiler_params=pltpu.CompilerParams(
            dimension_semantics=("parallel",), vmem_limit_bytes=VMEM_LIMIT_BYTES),
        name="in_proj",
    )(x2, g1, w_in, cos2, sin2, lng, lnb, ws, bs)


def _ret_state_body(gc_f, gc_b, n_blocks, kf_ref, vf_ref, kb_ref, vb_ref, zf_ref, zb_ref,
                    hmain_ref, hprev_ref, hnext_ref, cw_ref, cb_ref, clg_ref, clb_ref,
                    sf_ref, sb_ref, ycv_ref, st_ref, hpad_ref):
    i = pl.program_id(1)

    @pl.when(i == 0)
    def _():
        st_ref[...] = jnp.zeros_like(st_ref)

    rows_per_step = STATE_CHUNKS * CHUNK
    zero_halo = jnp.zeros((HALO_ROWS, CONV_WIDTH), F32)
    hpad_ref[0:HALO_ROWS, :] = jnp.where(i == 0, zero_halo, hprev_ref[...])
    hpad_ref[HALO_ROWS:HALO_ROWS + rows_per_step, :] = hmain_ref[...]
    hpad_ref[HALO_ROWS + rows_per_step:, :] = jnp.where(i == n_blocks - 1, zero_halo, hnext_ref[...])
    span = CONV_ROW_BLOCK + SUBLANES
    for rb in range(rows_per_step // CONV_ROW_BLOCK):
        r0 = rb * CONV_ROW_BLOCK
        acc = None
        for r in range(SUBLANES):
            part = None
            for j in range(CONV_KERNEL):
                if (j - CONV_PAD) % SUBLANES != r:
                    continue
                start = r0 + HALO_ROWS + (j - CONV_PAD - r)
                term = cw_ref[j:j + 1, :] * hpad_ref[start:start + span, :]
                part = term if part is None else part + term
            if r:
                part = pltpu.roll(part, span - r, 0)
            part = part[:CONV_ROW_BLOCK]
            acc = part if acc is None else acc + part
        hc = _standardize(acc + cb_ref[...]) * clg_ref[...] + clb_ref[...]
        ycv_ref[r0:r0 + CONV_ROW_BLOCK, :] = (hc * _sigmoid(hc)).astype(BF16)

    def chunk_kv(k_ref, v_ref, z_ref, j):
        rows = slice(j * CHUNK, (j + 1) * CHUNK)
        vz = (v_ref[rows, :].astype(F32) * z_ref[...]).astype(BF16)
        out = []
        for hd in range(RET_HEADS):
            cols = slice(hd * RET_HEAD_DIM, (hd + 1) * RET_HEAD_DIM)
            out.append(lax.dot_general(k_ref[rows, cols], vz[:, cols], (((0,), (0,)), ((), ())),
                                       preferred_element_type=F32))
        return out

    for d, (k_ref, v_ref, z_ref, s_ref, gc, order) in enumerate((
            (kf_ref, vf_ref, zf_ref, sf_ref, gc_f, range(STATE_CHUNKS)),
            (kb_ref, vb_ref, zb_ref, sb_ref, gc_b, range(STATE_CHUNKS - 1, -1, -1)))):
        for j in order:
            kv = chunk_kv(k_ref, v_ref, z_ref, j)
            for hd in range(RET_HEADS):
                state = st_ref[d, hd]
                s_ref[j, hd] = state.astype(BF16)
                st_ref[d, hd] = gc[hd] * state + kv[hd]


def _ret_state(k, v, zf, zb, gc_f, gc_b, hglu, cw, cb, clg, clb, batch, seq):
    rows = STATE_CHUNKS * CHUNK
    nb = seq // rows
    m = batch * seq
    n_chunks = m // CHUNK
    halo_per_step = rows // HALO_ROWS
    n_halo = m // HALO_ROWS
    step = lambda b, i: b * nb + i
    fwd = pl.BlockSpec((rows, RET_WIDTH), lambda b, i: (step(b, i), 0))
    bwd = pl.BlockSpec((rows, RET_WIDTH), lambda b, i: (b * nb + nb - 1 - i, 0))
    const = lambda shape: pl.BlockSpec(shape, lambda b, i: (0,) * len(shape))
    state_shape = (STATE_CHUNKS, RET_HEADS, RET_HEAD_DIM, RET_HEAD_DIM)
    return pl.pallas_call(
        functools.partial(_ret_state_body, gc_f, gc_b, nb),
        grid=(batch, nb),
        in_specs=[
            fwd, fwd, bwd, bwd, const((CHUNK, RET_WIDTH)), const((CHUNK, RET_WIDTH)),
            pl.BlockSpec((rows, CONV_WIDTH), lambda b, i: (step(b, i), 0)),
            pl.BlockSpec((HALO_ROWS, CONV_WIDTH),
                         lambda b, i: (jnp.maximum(step(b, i) * halo_per_step - 1, 0), 0)),
            pl.BlockSpec((HALO_ROWS, CONV_WIDTH),
                         lambda b, i: (jnp.minimum((step(b, i) + 1) * halo_per_step, n_halo - 1), 0)),
            const((CONV_KERNEL, CONV_WIDTH)), const((1, CONV_WIDTH)), const((1, CONV_WIDTH)),
            const((1, CONV_WIDTH)),
        ],
        out_specs=[
            pl.BlockSpec(state_shape, lambda b, i: (step(b, i), 0, 0, 0)),
            pl.BlockSpec(state_shape, lambda b, i: (b * nb + nb - 1 - i, 0, 0, 0)),
            pl.BlockSpec((rows, CONV_WIDTH), lambda b, i: (step(b, i), 0)),
        ],
        out_shape=[jax.ShapeDtypeStruct((n_chunks, RET_HEADS, RET_HEAD_DIM, RET_HEAD_DIM), BF16)] * 2
        + [jax.ShapeDtypeStruct((m, CONV_WIDTH), BF16)],
        scratch_shapes=[pltpu.VMEM((2, RET_HEADS, RET_HEAD_DIM, RET_HEAD_DIM), F32),
                        pltpu.VMEM((rows + 2 * HALO_ROWS, CONV_WIDTH), F32)],
        compiler_params=pltpu.CompilerParams(
            dimension_semantics=("parallel", "arbitrary"), vmem_limit_bytes=VMEM_LIMIT_BYTES),
        name="ret_state",
    )(k, v, k, v, zf, zb, hglu, hglu, hglu, cw, cb, clg, clb)


def _mix_ffn_body(final, x_ref, ygm_ref, q_ref, k_ref, v_ref, sg_ref, ycv_ref, sf_ref, sb_ref,
                  dmask_ref, xif_ref, xib_ref, wo_ref, g2_ref, win_ref, wdn_ref, gf_ref,
                  o_ref, p_ref, oret_ref, yret_ref, x1_ref):
    pairs = [(c, hd) for c in range(TOKEN_TILE // CHUNK) for hd in range(RET_HEADS)]

    def block(c, hd):
        return slice(c * CHUNK, (c + 1) * CHUNK), slice(hd * RET_HEAD_DIM, (hd + 1) * RET_HEAD_DIM)

    def masked_scores(i):
        c, hd = pairs[i]
        rows, cols = block(c, hd)
        scores = lax.dot_general(q_ref[rows, cols], k_ref[rows, cols], (((1,), (1,)), ((), ())),
                                 preferred_element_type=F32)
        p_ref[i] = (scores * dmask_ref[hd]).astype(BF16)

    def weighted_values(i):
        c, hd = pairs[i]
        rows, cols = block(c, hd)
        qh = q_ref[rows, cols]
        o = jnp.dot(p_ref[i], v_ref[rows, cols], preferred_element_type=F32)
        o = o + xif_ref[:, cols] * jnp.dot(qh, sf_ref[c, hd], preferred_element_type=F32)
        o = o + xib_ref[:, cols] * jnp.dot(qh, sb_ref[c, hd], preferred_element_type=F32)
        oret_ref[rows, cols] = o

    def head_norm_gate(hd):
        cols = slice(hd * RET_HEAD_DIM, (hd + 1) * RET_HEAD_DIM)
        yret_ref[:, cols] = (_standardize(oret_ref[:, cols])
                             * sg_ref[:, cols].astype(F32)).astype(BF16)

    x1_ref[...] = (x_ref[...]
                   + jnp.dot(ygm_ref[...], wo_ref[0:GM_WIDTH, :], preferred_element_type=F32)
                   + jnp.dot(ycv_ref[...], wo_ref[GM_WIDTH + RET_WIDTH:, :],
                             preferred_element_type=F32))
    for i in range(len(pairs)):
        masked_scores(i)
    for i in range(len(pairs)):
        weighted_values(i)
    for hd in range(RET_HEADS):
        head_norm_gate(hd)
    for n in range(D_MODEL // OUT_COL_BLOCK):
        cols = slice(n * OUT_COL_BLOCK, (n + 1) * OUT_COL_BLOCK)
        x1_ref[:, cols] += jnp.dot(yret_ref[...], wo_ref[GM_WIDTH:GM_WIDTH + RET_WIDTH, cols],
                                   preferred_element_type=F32)

    h = _rmsnorm(x1_ref[...], g2_ref[...]).astype(BF16)
    acc = None
    for j in range(FFN_HIDDEN // FFN_COL_BLOCK):
        lo = j * FFN_COL_BLOCK
        gate = jnp.dot(h, win_ref[:, lo:lo + FFN_COL_BLOCK], preferred_element_type=F32)
        up = jnp.dot(h, win_ref[:, FFN_HIDDEN + lo:FFN_HIDDEN + lo + FFN_COL_BLOCK],
                     preferred_element_type=F32)
        act = (gate * _sigmoid(gate) * up).astype(BF16)
        part = jnp.dot(act, wdn_ref[lo:lo + FFN_COL_BLOCK, :], preferred_element_type=F32)
        acc = part if acc is None else acc + part
    acc = x1_ref[...] + acc
    o_ref[...] = _rmsnorm(acc, gf_ref[...]) if final else acc


def _mix_ffn(x2, ygm, q, k, v, sg, ycv, sf, sb, dmask, xif, xib, wo, g2, win, wdn, layer, gf, final):
    m = x2.shape[0]
    tok = lambda width: pl.BlockSpec((TOKEN_TILE, width), lambda t: (t, 0))
    const = lambda shape: pl.BlockSpec(shape, lambda t: (0,) * len(shape))
    state = pl.BlockSpec((TOKEN_TILE // CHUNK, RET_HEADS, RET_HEAD_DIM, RET_HEAD_DIM),
                         lambda t: (t, 0, 0, 0))
    return pl.pallas_call(
        functools.partial(_mix_ffn_body, final),
        grid=(m // TOKEN_TILE,),
        in_specs=[
            tok(D_MODEL), tok(GM_WIDTH), tok(RET_WIDTH), tok(RET_WIDTH), tok(RET_WIDTH), tok(RET_WIDTH),
            tok(CONV_WIDTH), state, state,
            const((RET_HEADS, CHUNK, CHUNK)), const((CHUNK, RET_WIDTH)), const((CHUNK, RET_WIDTH)),
            _layer_weight(layer, D_MODEL, D_MODEL, 1), const((1, D_MODEL)),
            _layer_weight(layer, D_MODEL, 2 * FFN_HIDDEN, 1), _layer_weight(layer, FFN_HIDDEN, D_MODEL, 1),
            const((1, D_MODEL)),
        ],
        out_specs=tok(D_MODEL),
        out_shape=jax.ShapeDtypeStruct((m, D_MODEL), F32),
        scratch_shapes=[pltpu.VMEM((TOKEN_TILE // CHUNK * RET_HEADS, CHUNK, CHUNK), BF16),
                        pltpu.VMEM((TOKEN_TILE, RET_WIDTH), F32),
                        pltpu.VMEM((TOKEN_TILE, RET_WIDTH), BF16),
                        pltpu.VMEM((TOKEN_TILE, D_MODEL), F32)],
        compiler_params=pltpu.CompilerParams(
            dimension_semantics=("parallel",), vmem_limit_bytes=VMEM_LIMIT_BYTES),
        name="mix_ffn_final" if final else "mix_ffn",
    )(x2, ygm, q, k, v, sg, ycv, sf, sb, dmask, xif, xib, wo, g2, win, wdn, gf)


def kernel(x, norm1_g, w_in, gm_ln_g, gm_ln_b, gm_ws, gm_bs, conv_w, conv_b, conv_ln_g, conv_ln_b,
           w_out, norm2_g, w_ffn_in, w_ffn_out, final_g):
    batch, seq, _ = x.shape
    depth = w_in.shape[0]
    assert all(seq % t == 0 for t in (STATE_CHUNKS * CHUNK, TOKEN_TILE, PROJ_TILE))

    dmask, zf, zb, xif, xib, gc_f, gc_b = _retention_tables()
    cos2, sin2 = _rotary_tables(seq)
    row = lambda a: a.reshape(1, -1)

    w_in_b, w_out_b = w_in.astype(BF16), w_out.astype(BF16)
    w_ffn_in_b, w_ffn_out_b = w_ffn_in.astype(BF16), w_ffn_out.astype(BF16)

    x2 = x.reshape(batch * seq, D_MODEL)
    for l in range(depth):
        ws = gm_ws[l].reshape(GM_HEADS * CHUNK, CHUNK).astype(BF16)
        bs = jnp.repeat(gm_bs[l].T, GM_HEAD_DIM, axis=1)
        ygm, q, k, v, sg, hglu = _in_proj(
            x2, row(norm1_g[l]), w_in_b, l, cos2, sin2,
            row(gm_ln_g[l]), row(gm_ln_b[l]), ws, bs, seq)
        sf, sb, ycv = _ret_state(k, v, zf, zb, gc_f, gc_b, hglu, conv_w[l], row(conv_b[l]),
                                 row(conv_ln_g[l]), row(conv_ln_b[l]), batch, seq)
        x2 = _mix_ffn(x2, ygm, q, k, v, sg, ycv, sf, sb, dmask, xif, xib, w_out_b,
                      row(norm2_g[l]), w_ffn_in_b, w_ffn_out_b, l, row(final_g),
                      final=(l == depth - 1))
    return x2.reshape(batch, seq, D_MODEL)
```

```python
import functools

import numpy as np
import jax
import jax.numpy as jnp
from jax import lax
from jax.experimental import pallas as pl
from jax.experimental.pallas import tpu as pltpu

F32 = jnp.float32
BF16 = jnp.bfloat16

D_MODEL = 1024
GM_WIDTH = 256
GM_HEADS = 4
GM_HEAD_DIM = GM_WIDTH // GM_HEADS
CHUNK = 128
RET_WIDTH = 512
RET_HEADS = 4
RET_HEAD_DIM = RET_WIDTH // RET_HEADS
CONV_WIDTH = 256
CONV_KERNEL = 31
CONV_PAD = CONV_KERNEL // 2
IN_WIDTH = 2 * GM_WIDTH + 4 * RET_WIDTH + 2 * CONV_WIDTH
FFN_HIDDEN = 2816
ROPE_BASE = 10000.0
EPS = 1e-6

COL_GM = 0
COL_Q = 2 * GM_WIDTH
COL_V = COL_Q + 2 * RET_WIDTH
COL_CONV = COL_V + 2 * RET_WIDTH

TOKEN_TILE = 512
PROJ_TILE = 1024
HALO_ROWS = 16
STATE_CHUNKS = 16
CONV_ROW_BLOCK = 128
SUBLANES = 8
LANES = 128
FFN_COL_BLOCK = 256
OUT_COL_BLOCK = 256
VMEM_LIMIT_BYTES = 56 * 1024 * 1024


def _gelu(x):
    return 0.5 * x * (1.0 + lax.erf(x * np.float32(1.0 / np.sqrt(2.0))))


def _sigmoid(x):
    return 1.0 / (1.0 + jnp.exp(-x))


def _standardize(x):
    mu = jnp.mean(x, axis=-1, keepdims=True)
    xc = x - mu
    var = jnp.mean(xc * xc, axis=-1, keepdims=True)
    return xc * lax.rsqrt(var + EPS)


def _rmsnorm(x, g):
    return x * lax.rsqrt(jnp.mean(x * x, axis=-1, keepdims=True) + EPS) * g


def _retention_tables():
    idx = np.arange(CHUNK, dtype=np.float32)
    gamma_f = (1.0 - np.exp2(-5.0 - np.arange(RET_HEADS, dtype=np.float32))).astype(np.float32)
    gamma_b = gamma_f[::-1]
    lf = np.log(gamma_f)[:, None]
    lb = np.log(gamma_b)[:, None]
    diff = idx[:, None] - idx[None, :]
    dmask = np.where(diff >= 0,
                     np.exp(lf[:, :, None] * np.maximum(diff, 0.0)),
                     np.exp(lb[:, :, None] * np.maximum(-diff, 0.0))).astype(np.float32)
    zeta_f = np.exp(lf * (CHUNK - 1 - idx))
    zeta_b = np.exp(lb * idx)
    xi_f = np.exp(lf * (idx + 1))
    xi_b = np.exp(lb * (CHUNK - idx))

    def per_row(t):
        return np.repeat(t.T.astype(np.float32), RET_HEAD_DIM, axis=1)

    gc_f = [float(np.exp(np.float32(l) * np.float32(CHUNK))) for l in lf[:, 0]]
    gc_b = [float(np.exp(np.float32(l) * np.float32(CHUNK))) for l in lb[:, 0]]
    return dmask, per_row(zeta_f), per_row(zeta_b), per_row(xi_f), per_row(xi_b), gc_f, gc_b


def _rotary_tables(seq):
    half = RET_HEAD_DIM // 2
    inv_freq = ROPE_BASE ** (-np.arange(half, dtype=np.float64) / half)
    ang = np.arange(seq, dtype=np.float64)[:, None] * inv_freq[None, :]
    cos, sin = np.cos(ang), np.sin(ang)
    cos2 = np.concatenate([cos, cos], axis=-1).astype(np.float32)
    sin2 = np.concatenate([-sin, sin], axis=-1).astype(np.float32)
    return cos2, sin2


def _layer_weight(layer, rows, cols, buffers=2):
    return pl.BlockSpec((None, rows, cols), lambda t: (layer, 0, 0),
                        pipeline_mode=pl.Buffered(buffers))


def _in_proj_body(x_ref, g1_ref, w_ref, cos_ref, sin_ref, lng_ref, lnb_ref, ws_ref, bs_ref,
                  ygm_ref, q_ref, k_ref, v_ref, sg_ref, hglu_ref):
    h = _rmsnorm(x_ref[...], g1_ref[...]).astype(BF16)

    def proj(lo, hi):
        return jnp.dot(h, w_ref[:, lo:hi], preferred_element_type=F32)

    uv = proj(COL_GM, COL_Q)
    u = _gelu(uv[:, :GM_WIDTH])
    v = _gelu(uv[:, GM_WIDTH:])
    vln = (_standardize(v) * lng_ref[...] + lnb_ref[...]).astype(BF16)
    lane_head = lax.broadcasted_iota(jnp.int32, (CHUNK, GM_WIDTH), 1) // GM_HEAD_DIM
    for c in range(PROJ_TILE // CHUNK):
        rows = slice(c * CHUNK, (c + 1) * CHUNK)
        full = jnp.dot(ws_ref[...], vln[rows], preferred_element_type=F32)
        mixed = full[0:CHUNK]
        for hd in range(1, GM_HEADS):
            mixed = jnp.where(lane_head == hd, full[hd * CHUNK:(hd + 1) * CHUNK], mixed)
        ygm_ref[rows, :] = (u[rows] * (mixed + bs_ref[...])).astype(BF16)

    cos2 = cos_ref[...]
    sin2 = sin_ref[...]
    qk = proj(COL_Q, COL_V)
    scale = np.float32(RET_HEAD_DIM ** -0.5)
    for hd in range(RET_HEADS):
        cols = slice(hd * RET_HEAD_DIM, (hd + 1) * RET_HEAD_DIM)
        t = qk[:, cols]
        q_ref[:, cols] = (t * cos2 + pltpu.roll(t, RET_HEAD_DIM // 2, 1) * sin2).astype(BF16)
        t = qk[:, RET_WIDTH + hd * RET_HEAD_DIM:RET_WIDTH + (hd + 1) * RET_HEAD_DIM]
        k_ref[:, cols] = ((t * cos2 + pltpu.roll(t, RET_HEAD_DIM // 2, 1) * sin2) * scale).astype(BF16)
    vg = proj(COL_V, COL_CONV)
    v_ref[...] = vg[:, :RET_WIDTH].astype(BF16)
    g = vg[:, RET_WIDTH:]
    sg_ref[...] = (g * _sigmoid(g)).astype(BF16)

    ag = proj(COL_CONV, IN_WIDTH)
    hglu_ref[...] = ag[:, :CONV_WIDTH] * _sigmoid(ag[:, CONV_WIDTH:])


def _in_proj(x2, g1, w_in, layer, cos2, sin2, lng, lnb, ws, bs, seq):
    m = x2.shape[0]
    tiles_per_seq = seq // PROJ_TILE
    tok = lambda width: pl.BlockSpec((PROJ_TILE, width), lambda t: (t, 0))
    const = lambda shape: pl.BlockSpec(shape, lambda t: (0,) * len(shape))
    return pl.pallas_call(
        _in_proj_body,
        grid=(m // PROJ_TILE,),
        in_specs=[
            tok(D_MODEL), const((1, D_MODEL)), _layer_weight(layer, D_MODEL, IN_WIDTH),
            pl.BlockSpec((PROJ_TILE, RET_HEAD_DIM), lambda t: (t % tiles_per_seq, 0)),
            pl.BlockSpec((PROJ_TILE, RET_HEAD_DIM), lambda t: (t % tiles_per_seq, 0)),
            const((1, GM_WIDTH)), const((1, GM_WIDTH)),
            const((GM_HEADS * CHUNK, CHUNK)), const((CHUNK, GM_WIDTH)),
        ],
        out_specs=[tok(GM_WIDTH), tok(RET_WIDTH), tok(RET_WIDTH), tok(RET_WIDTH), tok(RET_WIDTH),
                   tok(CONV_WIDTH)],
        out_shape=[
            jax.ShapeDtypeStruct((m, GM_WIDTH), BF16),
            jax.ShapeDtypeStruct((m, RET_WIDTH), BF16),
            jax.ShapeDtypeStruct((m, RET_WIDTH), BF16),
            jax.ShapeDtypeStruct((m, RET_WIDTH), BF16),
            jax.ShapeDtypeStruct((m, RET_WIDTH), BF16),
            jax.ShapeDtypeStruct((m, CONV_WIDTH), F32),
        ],
        compiler_params=pltpu.CompilerParams(
            dimension_semantics=("parallel",), vmem_limit_bytes=VMEM_LIMIT_BYTES),
        name="in_proj",
    )(x2, g1, w_in, cos2, sin2, lng, lnb, ws, bs)


def _ret_state_body(gc_f, gc_b, n_blocks, kf_ref, vf_ref, kb_ref, vb_ref, zf_ref, zb_ref,
                    hmain_ref, hprev_ref, hnext_ref, cw_ref, cb_ref, clg_ref, clb_ref,
                    sf_ref, sb_ref, ycv_ref, st_ref, hpad_ref):
    i = pl.program_id(1)

    @pl.when(i == 0)
    def _():
        st_ref[...] = jnp.zeros_like(st_ref)

    rows_per_step = STATE_CHUNKS * CHUNK
    zero_halo = jnp.zeros((HALO_ROWS, CONV_WIDTH), F32)
    hpad_ref[0:HALO_ROWS, :] = jnp.where(i == 0, zero_halo, hprev_ref[...])
    hpad_ref[HALO_ROWS:HALO_ROWS + rows_per_step, :] = hmain_ref[...]
    hpad_ref[HALO_ROWS + rows_per_step:, :] = jnp.where(i == n_blocks - 1, zero_halo, hnext_ref[...])
    span = CONV_ROW_BLOCK + SUBLANES
    for rb in range(rows_per_step // CONV_ROW_BLOCK):
        r0 = rb * CONV_ROW_BLOCK
        halves = []
        for ch in range(CONV_WIDTH // LANES):
            cols = slice(ch * LANES, (ch + 1) * LANES)
            acc = None
            for r in range(SUBLANES):
                part = None
                for j in range(CONV_KERNEL):
                    if (j - CONV_PAD) % SUBLANES != r:
                        continue
                    start = r0 + HALO_ROWS + (j - CONV_PAD - r)
                    term = cw_ref[j:j + 1, cols] * hpad_ref[start:start + span, cols]
                    part = term if part is None else part + term
                if r:
                    part = pltpu.roll(part, span - r, 0)
                part = part[:CONV_ROW_BLOCK]
                acc = part if acc is None else acc + part
            halves.append(acc)
        acc = jnp.concatenate(halves, axis=1)
        hc = _standardize(acc + cb_ref[...]) * clg_ref[...] + clb_ref[...]
        ycv_ref[r0:r0 + CONV_ROW_BLOCK, :] = (hc * _sigmoid(hc)).astype(BF16)

    def chunk_kv(k_ref, v_ref, z_ref, j):
        rows = slice(j * CHUNK, (j + 1) * CHUNK)
        vz = (v_ref[rows, :].astype(F32) * z_ref[...]).astype(BF16)
        out = []
        for hd in range(RET_HEADS):
            cols = slice(hd * RET_HEAD_DIM, (hd + 1) * RET_HEAD_DIM)
            out.append(lax.dot_general(k_ref[rows, cols], vz[:, cols], (((0,), (0,)), ((), ())),
                                       preferred_element_type=F32))
        return out

    for d, (k_ref, v_ref, z_ref, s_ref, gc, order) in enumerate((
            (kf_ref, vf_ref, zf_ref, sf_ref, gc_f, range(STATE_CHUNKS)),
            (kb_ref, vb_ref, zb_ref, sb_ref, gc_b, range(STATE_CHUNKS - 1, -1, -1)))):
        for j in order:
            kv = chunk_kv(k_ref, v_ref, z_ref, j)
            for hd in range(RET_HEADS):
                state = st_ref[d, hd]
                s_ref[j, hd] = state.astype(BF16)
                st_ref[d, hd] = gc[hd] * state + kv[hd]


def _ret_state(k, v, zf, zb, gc_f, gc_b, hglu, cw, cb, clg, clb, batch, seq):
    rows = STATE_CHUNKS * CHUNK
    nb = seq // rows
    m = batch * seq
    n_chunks = m // CHUNK
    halo_per_step = rows // HALO_ROWS
    n_halo = m // HALO_ROWS
    step = lambda b, i: b * nb + i
    fwd = pl.BlockSpec((rows, RET_WIDTH), lambda b, i: (step(b, i), 0))
    bwd = pl.BlockSpec((rows, RET_WIDTH), lambda b, i: (b * nb + nb - 1 - i, 0))
    const = lambda shape: pl.BlockSpec(shape, lambda b, i: (0,) * len(shape))
    state_shape = (STATE_CHUNKS, RET_HEADS, RET_HEAD_DIM, RET_HEAD_DIM)
    return pl.pallas_call(
        functools.partial(_ret_state_body, gc_f, gc_b, nb),
        grid=(batch, nb),
        in_specs=[
            fwd, fwd, bwd, bwd, const((CHUNK, RET_WIDTH)), const((CHUNK, RET_WIDTH)),
            pl.BlockSpec((rows, CONV_WIDTH), lambda b, i: (step(b, i), 0)),
            pl.BlockSpec((HALO_ROWS, CONV_WIDTH),
                         lambda b, i: (jnp.maximum(step(b, i) * halo_per_step - 1, 0), 0)),
            pl.BlockSpec((HALO_ROWS, CONV_WIDTH),
                         lambda b, i: (jnp.minimum((step(b, i) + 1) * halo_per_step, n_halo - 1), 0)),
            const((CONV_KERNEL, CONV_WIDTH)), const((1, CONV_WIDTH)), const((1, CONV_WIDTH)),
            const((1, CONV_WIDTH)),
        ],
        out_specs=[
            pl.BlockSpec(state_shape, lambda b, i: (step(b, i), 0, 0, 0)),
            pl.BlockSpec(state_shape, lambda b, i: (b * nb + nb - 1 - i, 0, 0, 0)),
            pl.BlockSpec((rows, CONV_WIDTH), lambda b, i: (step(b, i), 0)),
        ],
        out_shape=[jax.ShapeDtypeStruct((n_chunks, RET_HEADS, RET_HEAD_DIM, RET_HEAD_DIM), BF16)] * 2
        + [jax.ShapeDtypeStruct((m, CONV_WIDTH), BF16)],
        scratch_shapes=[pltpu.VMEM((2, RET_HEADS, RET_HEAD_DIM, RET_HEAD_DIM), F32),
                        pltpu.VMEM((rows + 2 * HALO_ROWS, CONV_WIDTH), F32)],
        compiler_params=pltpu.CompilerParams(
            dimension_semantics=("parallel", "arbitrary"), vmem_limit_bytes=VMEM_LIMIT_BYTES),
        name="ret_state",
    )(k, v, k, v, zf, zb, hglu, hglu, hglu, cw, cb, clg, clb)


def _mix_ffn_body(final, x_ref, ygm_ref, q_ref, k_ref, v_ref, sg_ref, ycv_ref, sf_ref, sb_ref,
                  dmask_ref, xif_ref, xib_ref, wo_ref, g2_ref, win_ref, wdn_ref, gf_ref,
                  o_ref, p_ref, oret_ref, yret_ref, x1_ref):
    pairs = [(c, hd) for c in range(TOKEN_TILE // CHUNK) for hd in range(RET_HEADS)]

    def block(c, hd):
        return slice(c * CHUNK, (c + 1) * CHUNK), slice(hd * RET_HEAD_DIM, (hd + 1) * RET_HEAD_DIM)

    def masked_scores(i):
        c, hd = pairs[i]
        rows, cols = block(c, hd)
        scores = lax.dot_general(q_ref[rows, cols], k_ref[rows, cols], (((1,), (1,)), ((), ())),
                                 preferred_element_type=F32)
        p_ref[i] = (scores * dmask_ref[hd]).astype(BF16)

    def weighted_values(i):
        c, hd = pairs[i]
        rows, cols = block(c, hd)
        qh = q_ref[rows, cols]
        o = jnp.dot(p_ref[i], v_ref[rows, cols], preferred_element_type=F32)
        states = jnp.concatenate([sf_ref[c, hd], sb_ref[c, hd]], axis=1)
        cross = jnp.dot(qh, states, preferred_element_type=F32)
        o = o + xif_ref[:, cols] * cross[:, :RET_HEAD_DIM] + xib_ref[:, cols] * cross[:, RET_HEAD_DIM:]
        oret_ref[rows, cols] = o

    def head_norm_gate(hd):
        cols = slice(hd * RET_HEAD_DIM, (hd + 1) * RET_HEAD_DIM)
        yret_ref[:, cols] = (_standardize(oret_ref[:, cols])
                             * sg_ref[:, cols].astype(F32)).astype(BF16)

    x1_ref[...] = (x_ref[...]
                   + jnp.dot(ygm_ref[...], wo_ref[0:GM_WIDTH, :], preferred_element_type=F32)
                   + jnp.dot(ycv_ref[...], wo_ref[GM_WIDTH + RET_WIDTH:, :],
                             preferred_element_type=F32))
    for i in range(len(pairs)):
        masked_scores(i)
    for i in range(len(pairs)):
        weighted_values(i)
    for hd in range(RET_HEADS):
        head_norm_gate(hd)
    for n in range(D_MODEL // OUT_COL_BLOCK):
        cols = slice(n * OUT_COL_BLOCK, (n + 1) * OUT_COL_BLOCK)
        x1_ref[:, cols] += jnp.dot(yret_ref[...], wo_ref[GM_WIDTH:GM_WIDTH + RET_WIDTH, cols],
                                   preferred_element_type=F32)

    h = _rmsnorm(x1_ref[...], g2_ref[...]).astype(BF16)
    acc = None
    for j in range(FFN_HIDDEN // FFN_COL_BLOCK):
        lo = j * FFN_COL_BLOCK
        gate = jnp.dot(h, win_ref[:, lo:lo + FFN_COL_BLOCK], preferred_element_type=F32)
        up = jnp.dot(h, win_ref[:, FFN_HIDDEN + lo:FFN_HIDDEN + lo + FFN_COL_BLOCK],
                     preferred_element_type=F32)
        act = (gate * _sigmoid(gate) * up).astype(BF16)
        part = jnp.dot(act, wdn_ref[lo:lo + FFN_COL_BLOCK, :], preferred_element_type=F32)
        acc = part if acc is None else acc + part
    acc = x1_ref[...] + acc
    o_ref[...] = _rmsnorm(acc, gf_ref[...]) if final else acc


def _mix_ffn(x2, ygm, q, k, v, sg, ycv, sf, sb, dmask, xif, xib, wo, g2, win, wdn, layer, gf, final):
    m = x2.shape[0]
    tok = lambda width: pl.BlockSpec((TOKEN_TILE, width), lambda t: (t, 0))
    const = lambda shape: pl.BlockSpec(shape, lambda t: (0,) * len(shape))
    state = pl.BlockSpec((TOKEN_TILE // CHUNK, RET_HEADS, RET_HEAD_DIM, RET_HEAD_DIM),
                         lambda t: (t, 0, 0, 0))
    return pl.pallas_call(
        functools.partial(_mix_ffn_body, final),
        grid=(m // TOKEN_TILE,),
        in_specs=[
            tok(D_MODEL), tok(GM_WIDTH), tok(RET_WIDTH), tok(RET_WIDTH), tok(RET_WIDTH), tok(RET_WIDTH),
            tok(CONV_WIDTH), state, state,
            const((RET_HEADS, CHUNK, CHUNK)), const((CHUNK, RET_WIDTH)), const((CHUNK, RET_WIDTH)),
            _layer_weight(layer, D_MODEL, D_MODEL, 1), const((1, D_MODEL)),
            _layer_weight(layer, D_MODEL, 2 * FFN_HIDDEN, 1), _layer_weight(layer, FFN_HIDDEN, D_MODEL, 1),
            const((1, D_MODEL)),
        ],
        out_specs=tok(D_MODEL),
        out_shape=jax.ShapeDtypeStruct((m, D_MODEL), F32),
        scratch_shapes=[pltpu.VMEM((TOKEN_TILE // CHUNK * RET_HEADS, CHUNK, CHUNK), BF16),
                        pltpu.VMEM((TOKEN_TILE, RET_WIDTH), F32),
                        pltpu.VMEM((TOKEN_TILE, RET_WIDTH), BF16),
                        pltpu.VMEM((TOKEN_TILE, D_MODEL), F32)],
        compiler_params=pltpu.CompilerParams(
            dimension_semantics=("parallel",), vmem_limit_bytes=VMEM_LIMIT_BYTES),
        name="mix_ffn_final" if final else "mix_ffn",
    )(x2, ygm, q, k, v, sg, ycv, sf, sb, dmask, xif, xib, wo, g2, win, wdn, gf)


def kernel(x, norm1_g, w_in, gm_ln_g, gm_ln_b, gm_ws, gm_bs, conv_w, conv_b, conv_ln_g, conv_ln_b,
           w_out, norm2_g, w_ffn_in, w_ffn_out, final_g):
    batch, seq, _ = x.shape
    depth = w_in.shape[0]
    assert all(seq % t == 0 for t in (STATE_CHUNKS * CHUNK, TOKEN_TILE, PROJ_TILE))

    dmask, zf, zb, xif, xib, gc_f, gc_b = _retention_tables()
    cos2, sin2 = _rotary_tables(seq)
    row = lambda a: a.reshape(1, -1)

    w_in_b, w_out_b = w_in.astype(BF16), w_out.astype(BF16)
    w_ffn_in_b, w_ffn_out_b = w_ffn_in.astype(BF16), w_ffn_out.astype(BF16)

    x2 = x.reshape(batch * seq, D_MODEL)
    for l in range(depth):
        ws = gm_ws[l].reshape(GM_HEADS * CHUNK, CHUNK).astype(BF16)
        bs = jnp.repeat(gm_bs[l].T, GM_HEAD_DIM, axis=1)
        ygm, q, k, v, sg, hglu = _in_proj(
            x2, row(norm1_g[l]), w_in_b, l, cos2, sin2,
            row(gm_ln_g[l]), row(gm_ln_b[l]), ws, bs, seq)
        sf, sb, ycv = _ret_state(k, v, zf, zb, gc_f, gc_b, hglu, conv_w[l], row(conv_b[l]),
                                 row(conv_ln_g[l]), row(conv_ln_b[l]), batch, seq)
        x2 = _mix_ffn(x2, ygm, q, k, v, sg, ycv, sf, sb, dmask, xif, xib, w_out_b,
                      row(norm2_g[l]), w_ffn_in_b, w_ffn_out_b, l, row(final_g),
                      final=(l == depth - 1))
    return x2.reshape(batch, seq, D_MODEL)
```

```python
import functools

import numpy as np
import jax
import jax.numpy as jnp
from jax import lax
from jax.experimental import pallas as pl
from jax.experimental.pallas import tpu as pltpu

F32 = jnp.float32
BF16 = jnp.bfloat16

D_MODEL = 1024
GM_WIDTH = 256
GM_HEADS = 4
GM_HEAD_DIM = GM_WIDTH // GM_HEADS
CHUNK = 128
RET_WIDTH = 512
RET_HEADS = 4
RET_HEAD_DIM = RET_WIDTH // RET_HEADS
CONV_WIDTH = 256
CONV_KERNEL = 31
CONV_PAD = CONV_KERNEL // 2
IN_WIDTH = 2 * GM_WIDTH + 4 * RET_WIDTH + 2 * CONV_WIDTH
FFN_HIDDEN = 2816
ROPE_BASE = 10000.0
EPS = 1e-6

COL_GM = 0
COL_Q = 2 * GM_WIDTH
COL_V = COL_Q + 2 * RET_WIDTH
COL_CONV = COL_V + 2 * RET_WIDTH

TOKEN_TILE = 512
PROJ_TILE = 1024
HALO_ROWS = 16
STATE_CHUNKS = 16
CONV_ROW_BLOCK = 128
SUBLANES = 8
LANES = 128
FFN_COL_BLOCK = 512
OUT_COL_BLOCK = 256
VMEM_LIMIT_BYTES = 56 * 1024 * 1024


def _gelu(x):
    return 0.5 * x * (1.0 + lax.erf(x * np.float32(1.0 / np.sqrt(2.0))))


def _sigmoid(x):
    return 1.0 / (1.0 + jnp.exp(-x))


def _standardize(x):
    mu = jnp.mean(x, axis=-1, keepdims=True)
    xc = x - mu
    var = jnp.mean(xc * xc, axis=-1, keepdims=True)
    return xc * lax.rsqrt(var + EPS)


def _rmsnorm(x, g):
    return x * lax.rsqrt(jnp.mean(x * x, axis=-1, keepdims=True) + EPS) * g


def _retention_tables():
    idx = np.arange(CHUNK, dtype=np.float32)
    gamma_f = (1.0 - np.exp2(-5.0 - np.arange(RET_HEADS, dtype=np.float32))).astype(np.float32)
    gamma_b = gamma_f[::-1]
    lf = np.log(gamma_f)[:, None]
    lb = np.log(gamma_b)[:, None]
    diff = idx[:, None] - idx[None, :]
    dmask = np.where(diff >= 0,
                     np.exp(lf[:, :, None] * np.maximum(diff, 0.0)),
                     np.exp(lb[:, :, None] * np.maximum(-diff, 0.0))).astype(np.float32)
    zeta_f = np.exp(lf * (CHUNK - 1 - idx))
    zeta_b = np.exp(lb * idx)
    xi_f = np.exp(lf * (idx + 1))
    xi_b = np.exp(lb * (CHUNK - idx))

    def per_row(t):
        return np.repeat(t.T.astype(np.float32), RET_HEAD_DIM, axis=1)

    gc_f = [float(np.exp(np.float32(l) * np.float32(CHUNK))) for l in lf[:, 0]]
    gc_b = [float(np.exp(np.float32(l) * np.float32(CHUNK))) for l in lb[:, 0]]
    return dmask, per_row(zeta_f), per_row(zeta_b), per_row(xi_f), per_row(xi_b), gc_f, gc_b


def _rotary_tables(seq):
    half = RET_HEAD_DIM // 2
    inv_freq = ROPE_BASE ** (-np.arange(half, dtype=np.float64) / half)
    ang = np.arange(seq, dtype=np.float64)[:, None] * inv_freq[None, :]
    cos, sin = np.cos(ang), np.sin(ang)
    cos2 = np.concatenate([cos, cos], axis=-1).astype(np.float32)
    sin2 = np.concatenate([-sin, sin], axis=-1).astype(np.float32)
    return cos2, sin2


def _layer_weight(layer, rows, cols, buffers=2):
    return pl.BlockSpec((None, rows, cols), lambda t: (layer, 0, 0),
                        pipeline_mode=pl.Buffered(buffers))


def _in_proj_body(x_ref, g1_ref, w_ref, cos_ref, sin_ref, lng_ref, lnb_ref, ws_ref, bs_ref,
                  ygm_ref, q_ref, k_ref, v_ref, sg_ref, hglu_ref):
    h = _rmsnorm(x_ref[...], g1_ref[...]).astype(BF16)

    def proj(lo, hi):
        return jnp.dot(h, w_ref[:, lo:hi], preferred_element_type=F32)

    uv = proj(COL_GM, COL_Q)
    u = _gelu(uv[:, :GM_WIDTH])
    v = _gelu(uv[:, GM_WIDTH:])
    vln = (_standardize(v) * lng_ref[...] + lnb_ref[...]).astype(BF16)
    lane_head = lax.broadcasted_iota(jnp.int32, (CHUNK, GM_WIDTH), 1) // GM_HEAD_DIM
    for c in range(PROJ_TILE // CHUNK):
        rows = slice(c * CHUNK, (c + 1) * CHUNK)
        full = jnp.dot(ws_ref[...], vln[rows], preferred_element_type=F32)
        mixed = full[0:CHUNK]
        for hd in range(1, GM_HEADS):
            mixed = jnp.where(lane_head == hd, full[hd * CHUNK:(hd + 1) * CHUNK], mixed)
        ygm_ref[rows, :] = (u[rows] * (mixed + bs_ref[...])).astype(BF16)

    cos2 = cos_ref[...]
    sin2 = sin_ref[...]
    qk = proj(COL_Q, COL_V)
    scale = np.float32(RET_HEAD_DIM ** -0.5)
    for hd in range(RET_HEADS):
        cols = slice(hd * RET_HEAD_DIM, (hd + 1) * RET_HEAD_DIM)
        t = qk[:, cols]
        q_ref[:, cols] = (t * cos2 + pltpu.roll(t, RET_HEAD_DIM // 2, 1) * sin2).astype(BF16)
        t = qk[:, RET_WIDTH + hd * RET_HEAD_DIM:RET_WIDTH + (hd + 1) * RET_HEAD_DIM]
        k_ref[:, cols] = ((t * cos2 + pltpu.roll(t, RET_HEAD_DIM // 2, 1) * sin2) * scale).astype(BF16)
    vg = proj(COL_V, COL_CONV)
    v_ref[...] = vg[:, :RET_WIDTH].astype(BF16)
    g = vg[:, RET_WIDTH:]
    sg_ref[...] = (g * _sigmoid(g)).astype(BF16)

    ag = proj(COL_CONV, IN_WIDTH)
    hglu_ref[...] = ag[:, :CONV_WIDTH] * _sigmoid(ag[:, CONV_WIDTH:])


def _in_proj(x2, g1, w_in, layer, cos2, sin2, lng, lnb, ws, bs, seq):
    m = x2.shape[0]
    tiles_per_seq = seq // PROJ_TILE
    tok = lambda width: pl.BlockSpec((PROJ_TILE, width), lambda t: (t, 0))
    const = lambda shape: pl.BlockSpec(shape, lambda t: (0,) * len(shape))
    return pl.pallas_call(
        _in_proj_body,
        grid=(m // PROJ_TILE,),
        in_specs=[
            tok(D_MODEL), const((1, D_MODEL)), _layer_weight(layer, D_MODEL, IN_WIDTH),
            pl.BlockSpec((PROJ_TILE, RET_HEAD_DIM), lambda t: (t % tiles_per_seq, 0)),
            pl.BlockSpec((PROJ_TILE, RET_HEAD_DIM), lambda t: (t % tiles_per_seq, 0)),
            const((1, GM_WIDTH)), const((1, GM_WIDTH)),
            const((GM_HEADS * CHUNK, CHUNK)), const((CHUNK, GM_WIDTH)),
        ],
        out_specs=[tok(GM_WIDTH), tok(RET_WIDTH), tok(RET_WIDTH), tok(RET_WIDTH), tok(RET_WIDTH),
                   tok(CONV_WIDTH)],
        out_shape=[
            jax.ShapeDtypeStruct((m, GM_WIDTH), BF16),
            jax.ShapeDtypeStruct((m, RET_WIDTH), BF16),
            jax.ShapeDtypeStruct((m, RET_WIDTH), BF16),
            jax.ShapeDtypeStruct((m, RET_WIDTH), BF16),
            jax.ShapeDtypeStruct((m, RET_WIDTH), BF16),
            jax.ShapeDtypeStruct((m, CONV_WIDTH), F32),
        ],
        compiler_params=pltpu.CompilerParams(
            dimension_semantics=("parallel",), vmem_limit_bytes=VMEM_LIMIT_BYTES),
        name="in_proj",
    )(x2, g1, w_in, cos2, sin2, lng, lnb, ws, bs)


def _ret_state_body(gc_f, gc_b, n_blocks, kf_ref, vf_ref, kb_ref, vb_ref, zf_ref, zb_ref,
                    hmain_ref, hprev_ref, hnext_ref, cw_ref, cb_ref, clg_ref, clb_ref,
                    sf_ref, sb_ref, ycv_ref, st_ref, hpad_ref):
    i = pl.program_id(1)

    @pl.when(i == 0)
    def _():
        st_ref[...] = jnp.zeros_like(st_ref)

    rows_per_step = STATE_CHUNKS * CHUNK
    zero_halo = jnp.zeros((HALO_ROWS, CONV_WIDTH), F32)
    hpad_ref[0:HALO_ROWS, :] = jnp.where(i == 0, zero_halo, hprev_ref[...])
    hpad_ref[HALO_ROWS:HALO_ROWS + rows_per_step, :] = hmain_ref[...]
    hpad_ref[HALO_ROWS + rows_per_step:, :] = jnp.where(i == n_blocks - 1, zero_halo, hnext_ref[...])
    span = CONV_ROW_BLOCK + SUBLANES
    for rb in range(rows_per_step // CONV_ROW_BLOCK):
        r0 = rb * CONV_ROW_BLOCK
        halves = []
        for ch in range(CONV_WIDTH // LANES):
            cols = slice(ch * LANES, (ch + 1) * LANES)
            acc = None
            for r in range(SUBLANES):
                part = None
                for j in range(CONV_KERNEL):
                    if (j - CONV_PAD) % SUBLANES != r:
                        continue
                    start = r0 + HALO_ROWS + (j - CONV_PAD - r)
                    term = cw_ref[j:j + 1, cols] * hpad_ref[start:start + span, cols]
                    part = term if part is None else part + term
                if r:
                    part = pltpu.roll(part, span - r, 0)
                part = part[:CONV_ROW_BLOCK]
                acc = part if acc is None else acc + part
            halves.append(acc)
        acc = jnp.concatenate(halves, axis=1)
        hc = _standardize(acc + cb_ref[...]) * clg_ref[...] + clb_ref[...]
        ycv_ref[r0:r0 + CONV_ROW_BLOCK, :] = (hc * _sigmoid(hc)).astype(BF16)

    def chunk_kv(k_ref, v_ref, z_ref, j):
        rows = slice(j * CHUNK, (j + 1) * CHUNK)
        vz = (v_ref[rows, :].astype(F32) * z_ref[...]).astype(BF16)
        out = []
        for hd in range(RET_HEADS):
            cols = slice(hd * RET_HEAD_DIM, (hd + 1) * RET_HEAD_DIM)
            out.append(lax.dot_general(k_ref[rows, cols], vz[:, cols], (((0,), (0,)), ((), ())),
                                       preferred_element_type=F32))
        return out

    for d, (k_ref, v_ref, z_ref, s_ref, gc, order) in enumerate((
            (kf_ref, vf_ref, zf_ref, sf_ref, gc_f, range(STATE_CHUNKS)),
            (kb_ref, vb_ref, zb_ref, sb_ref, gc_b, range(STATE_CHUNKS - 1, -1, -1)))):
        for j in order:
            kv = chunk_kv(k_ref, v_ref, z_ref, j)
            for hd in range(RET_HEADS):
                state = st_ref[d, hd]
                s_ref[j, hd] = state.astype(BF16)
                st_ref[d, hd] = gc[hd] * state + kv[hd]


def _ret_state(k, v, zf, zb, gc_f, gc_b, hglu, cw, cb, clg, clb, batch, seq):
    rows = STATE_CHUNKS * CHUNK
    nb = seq // rows
    m = batch * seq
    n_chunks = m // CHUNK
    halo_per_step = rows // HALO_ROWS
    n_halo = m // HALO_ROWS
    step = lambda b, i: b * nb + i
    fwd = pl.BlockSpec((rows, RET_WIDTH), lambda b, i: (step(b, i), 0))
    bwd = pl.BlockSpec((rows, RET_WIDTH), lambda b, i: (b * nb + nb - 1 - i, 0))
    const = lambda shape: pl.BlockSpec(shape, lambda b, i: (0,) * len(shape))
    state_shape = (STATE_CHUNKS, RET_HEADS, RET_HEAD_DIM, RET_HEAD_DIM)
    return pl.pallas_call(
        functools.partial(_ret_state_body, gc_f, gc_b, nb),
        grid=(batch, nb),
        in_specs=[
            fwd, fwd, bwd, bwd, const((CHUNK, RET_WIDTH)), const((CHUNK, RET_WIDTH)),
            pl.BlockSpec((rows, CONV_WIDTH), lambda b, i: (step(b, i), 0)),
            pl.BlockSpec((HALO_ROWS, CONV_WIDTH),
                         lambda b, i: (jnp.maximum(step(b, i) * halo_per_step - 1, 0), 0)),
            pl.BlockSpec((HALO_ROWS, CONV_WIDTH),
                         lambda b, i: (jnp.minimum((step(b, i) + 1) * halo_per_step, n_halo - 1), 0)),
            const((CONV_KERNEL, CONV_WIDTH)), const((1, CONV_WIDTH)), const((1, CONV_WIDTH)),
            const((1, CONV_WIDTH)),
        ],
        out_specs=[
            pl.BlockSpec(state_shape, lambda b, i: (step(b, i), 0, 0, 0)),
            pl.BlockSpec(state_shape, lambda b, i: (b * nb + nb - 1 - i, 0, 0, 0)),
            pl.BlockSpec((rows, CONV_WIDTH), lambda b, i: (step(b, i), 0)),
        ],
        out_shape=[jax.ShapeDtypeStruct((n_chunks, RET_HEADS, RET_HEAD_DIM, RET_HEAD_DIM), BF16)] * 2
        + [jax.ShapeDtypeStruct((m, CONV_WIDTH), BF16)],
        scratch_shapes=[pltpu.VMEM((2, RET_HEADS, RET_HEAD_DIM, RET_HEAD_DIM), F32),
                        pltpu.VMEM((rows + 2 * HALO_ROWS, CONV_WIDTH), F32)],
        compiler_params=pltpu.CompilerParams(
            dimension_semantics=("parallel", "arbitrary"), vmem_limit_bytes=VMEM_LIMIT_BYTES),
        name="ret_state",
    )(k, v, k, v, zf, zb, hglu, hglu, hglu, cw, cb, clg, clb)


def _mix_ffn_body(final, x_ref, ygm_ref, q_ref, k_ref, v_ref, sg_ref, ycv_ref, sf_ref, sb_ref,
                  dmask_ref, xif_ref, xib_ref, wo_ref, g2_ref, win_ref, wdn_ref, gf_ref,
                  o_ref, p_ref, oret_ref, yret_ref, x1_ref):
    pairs = [(c, hd) for c in range(TOKEN_TILE // CHUNK) for hd in range(RET_HEADS)]

    def block(c, hd):
        return slice(c * CHUNK, (c + 1) * CHUNK), slice(hd * RET_HEAD_DIM, (hd + 1) * RET_HEAD_DIM)

    def masked_scores(i):
        c, hd = pairs[i]
        rows, cols = block(c, hd)
        scores = lax.dot_general(q_ref[rows, cols], k_ref[rows, cols], (((1,), (1,)), ((), ())),
                                 preferred_element_type=F32)
        p_ref[i] = (scores * dmask_ref[hd]).astype(BF16)

    def weighted_values(i):
        c, hd = pairs[i]
        rows, cols = block(c, hd)
        qh = q_ref[rows, cols]
        o = jnp.dot(p_ref[i], v_ref[rows, cols], preferred_element_type=F32)
        states = jnp.concatenate([sf_ref[c, hd], sb_ref[c, hd]], axis=1)
        cross = jnp.dot(qh, states, preferred_element_type=F32)
        o = o + xif_ref[:, cols] * cross[:, :RET_HEAD_DIM] + xib_ref[:, cols] * cross[:, RET_HEAD_DIM:]
        oret_ref[rows, cols] = o

    def head_norm_gate(hd):
        cols = slice(hd * RET_HEAD_DIM, (hd + 1) * RET_HEAD_DIM)
        yret_ref[:, cols] = (_standardize(oret_ref[:, cols])
                             * sg_ref[:, cols].astype(F32)).astype(BF16)

    x1_ref[...] = (x_ref[...]
                   + jnp.dot(ygm_ref[...], wo_ref[0:GM_WIDTH, :], preferred_element_type=F32)
                   + jnp.dot(ycv_ref[...], wo_ref[GM_WIDTH + RET_WIDTH:, :],
                             preferred_element_type=F32))
    for i in range(len(pairs)):
        masked_scores(i)
    for i in range(len(pairs)):
        weighted_values(i)
    for hd in range(RET_HEADS):
        head_norm_gate(hd)
    for n in range(D_MODEL // OUT_COL_BLOCK):
        cols = slice(n * OUT_COL_BLOCK, (n + 1) * OUT_COL_BLOCK)
        x1_ref[:, cols] += jnp.dot(yret_ref[...], wo_ref[GM_WIDTH:GM_WIDTH + RET_WIDTH, cols],
                                   preferred_element_type=F32)

    h = _rmsnorm(x1_ref[...], g2_ref[...]).astype(BF16)
    acc = None
    for lo in range(0, FFN_HIDDEN, FFN_COL_BLOCK):
        hi = min(lo + FFN_COL_BLOCK, FFN_HIDDEN)
        gate = jnp.dot(h, win_ref[:, lo:hi], preferred_element_type=F32)
        up = jnp.dot(h, win_ref[:, FFN_HIDDEN + lo:FFN_HIDDEN + hi], preferred_element_type=F32)
        act = (gate * _sigmoid(gate) * up).astype(BF16)
        part = jnp.dot(act, wdn_ref[lo:hi, :], preferred_element_type=F32)
        acc = part if acc is None else acc + part
    acc = x1_ref[...] + acc
    o_ref[...] = _rmsnorm(acc, gf_ref[...]) if final else acc


def _mix_ffn(x2, ygm, q, k, v, sg, ycv, sf, sb, dmask, xif, xib, wo, g2, win, wdn, layer, gf, final):
    m = x2.shape[0]
    tok = lambda width: pl.BlockSpec((TOKEN_TILE, width), lambda t: (t, 0))
    const = lambda shape: pl.BlockSpec(shape, lambda t: (0,) * len(shape))
    state = pl.BlockSpec((TOKEN_TILE // CHUNK, RET_HEADS, RET_HEAD_DIM, RET_HEAD_DIM),
                         lambda t: (t, 0, 0, 0))
    return pl.pallas_call(
        functools.partial(_mix_ffn_body, final),
        grid=(m // TOKEN_TILE,),
        in_specs=[
            tok(D_MODEL), tok(GM_WIDTH), tok(RET_WIDTH), tok(RET_WIDTH), tok(RET_WIDTH), tok(RET_WIDTH),
            tok(CONV_WIDTH), state, state,
            const((RET_HEADS, CHUNK, CHUNK)), const((CHUNK, RET_WIDTH)), const((CHUNK, RET_WIDTH)),
            _layer_weight(layer, D_MODEL, D_MODEL, 1), const((1, D_MODEL)),
            _layer_weight(layer, D_MODEL, 2 * FFN_HIDDEN, 1), _layer_weight(layer, FFN_HIDDEN, D_MODEL, 1),
            const((1, D_MODEL)),
        ],
        out_specs=tok(D_MODEL),
        out_shape=jax.ShapeDtypeStruct((m, D_MODEL), F32),
        scratch_shapes=[pltpu.VMEM((TOKEN_TILE // CHUNK * RET_HEADS, CHUNK, CHUNK), BF16),
                        pltpu.VMEM((TOKEN_TILE, RET_WIDTH), F32),
                        pltpu.VMEM((TOKEN_TILE, RET_WIDTH), BF16),
                        pltpu.VMEM((TOKEN_TILE, D_MODEL), F32)],
        compiler_params=pltpu.CompilerParams(
            dimension_semantics=("parallel",), vmem_limit_bytes=VMEM_LIMIT_BYTES),
        name="mix_ffn_final" if final else "mix_ffn",
    )(x2, ygm, q, k, v, sg, ycv, sf, sb, dmask, xif, xib, wo, g2, win, wdn, gf)


def kernel(x, norm1_g, w_in, gm_ln_g, gm_ln_b, gm_ws, gm_bs, conv_w, conv_b, conv_ln_g, conv_ln_b,
           w_out, norm2_g, w_ffn_in, w_ffn_out, final_g):
    batch, seq, _ = x.shape
    depth = w_in.shape[0]
    assert all(seq % t == 0 for t in (STATE_CHUNKS * CHUNK, TOKEN_TILE, PROJ_TILE))

    dmask, zf, zb, xif, xib, gc_f, gc_b = _retention_tables()
    cos2, sin2 = _rotary_tables(seq)
    row = lambda a: a.reshape(1, -1)

    w_in_b, w_out_b = w_in.astype(BF16), w_out.astype(BF16)
    w_ffn_in_b, w_ffn_out_b = w_ffn_in.astype(BF16), w_ffn_out.astype(BF16)

    x2 = x.reshape(batch * seq, D_MODEL)
    for l in range(depth):
        ws = gm_ws[l].reshape(GM_HEADS * CHUNK, CHUNK).astype(BF16)
        bs = jnp.repeat(gm_bs[l].T, GM_HEAD_DIM, axis=1)
        ygm, q, k, v, sg, hglu = _in_proj(
            x2, row(norm1_g[l]), w_in_b, l, cos2, sin2,
            row(gm_ln_g[l]), row(gm_ln_b[l]), ws, bs, seq)
        sf, sb, ycv = _ret_state(k, v, zf, zb, gc_f, gc_b, hglu, conv_w[l], row(conv_b[l]),
                                 row(conv_ln_g[l]), row(conv_ln_b[l]), batch, seq)
        x2 = _mix_ffn(x2, ygm, q, k, v, sg, ycv, sf, sb, dmask, xif, xib, w_out_b,
                      row(norm2_g[l]), w_ffn_in_b, w_ffn_out_b, l, row(final_g),
                      final=(l == depth - 1))
    return x2.reshape(batch, seq, D_MODEL)
```

```python
import functools

import numpy as np
import jax
import jax.numpy as jnp
from jax import lax
from jax.experimental import pallas as pl
from jax.experimental.pallas import tpu as pltpu

F32 = jnp.float32
BF16 = jnp.bfloat16

D_MODEL = 1024
GM_WIDTH = 256
GM_HEADS = 4
GM_HEAD_DIM = GM_WIDTH // GM_HEADS
CHUNK = 128
RET_WIDTH = 512
RET_HEADS = 4
RET_HEAD_DIM = RET_WIDTH // RET_HEADS
CONV_WIDTH = 256
CONV_KERNEL = 31
CONV_PAD = CONV_KERNEL // 2
IN_WIDTH = 2 * GM_WIDTH + 4 * RET_WIDTH + 2 * CONV_WIDTH
FFN_HIDDEN = 2816
ROPE_BASE = 10000.0
EPS = 1e-6

COL_GM = 0
COL_Q = 2 * GM_WIDTH
COL_V = COL_Q + 2 * RET_WIDTH
COL_CONV = COL_V + 2 * RET_WIDTH

TOKEN_TILE = 512
PROJ_TILE = 1024
HALO_ROWS = 16
STATE_CHUNKS = 16
CONV_ROW_BLOCK = 128
SUBLANES = 8
LANES = 128
FFN_COL_BLOCK = 512
VMEM_LIMIT_BYTES = 56 * 1024 * 1024


def _gelu(x):
    return 0.5 * x * (1.0 + lax.erf(x * np.float32(1.0 / np.sqrt(2.0))))


def _sigmoid(x):
    return 1.0 / (1.0 + jnp.exp(-x))


def _standardize(x):
    mu = jnp.mean(x, axis=-1, keepdims=True)
    xc = x - mu
    var = jnp.mean(xc * xc, axis=-1, keepdims=True)
    return xc * lax.rsqrt(var + EPS)


def _rmsnorm(x, g):
    return x * lax.rsqrt(jnp.mean(x * x, axis=-1, keepdims=True) + EPS) * g


def _retention_tables():
    idx = np.arange(CHUNK, dtype=np.float32)
    gamma_f = (1.0 - np.exp2(-5.0 - np.arange(RET_HEADS, dtype=np.float32))).astype(np.float32)
    gamma_b = gamma_f[::-1]
    lf = np.log(gamma_f)[:, None]
    lb = np.log(gamma_b)[:, None]
    diff = idx[:, None] - idx[None, :]
    dmask = np.where(diff >= 0,
                     np.exp(lf[:, :, None] * np.maximum(diff, 0.0)),
                     np.exp(lb[:, :, None] * np.maximum(-diff, 0.0))).astype(np.float32)
    zeta_f = np.exp(lf * (CHUNK - 1 - idx))
    zeta_b = np.exp(lb * idx)
    xi_f = np.exp(lf * (idx + 1))
    xi_b = np.exp(lb * (CHUNK - idx))

    def per_row(t):
        return np.repeat(t.T.astype(np.float32), RET_HEAD_DIM, axis=1)

    gc_f = [float(np.exp(np.float32(l) * np.float32(CHUNK))) for l in lf[:, 0]]
    gc_b = [float(np.exp(np.float32(l) * np.float32(CHUNK))) for l in lb[:, 0]]
    return dmask, per_row(zeta_f), per_row(zeta_b), per_row(xi_f), per_row(xi_b), gc_f, gc_b


def _rotary_tables(seq):
    half = RET_HEAD_DIM // 2
    inv_freq = ROPE_BASE ** (-np.arange(half, dtype=np.float64) / half)
    ang = np.arange(seq, dtype=np.float64)[:, None] * inv_freq[None, :]
    cos, sin = np.cos(ang), np.sin(ang)
    cos2 = np.concatenate([cos, cos], axis=-1).astype(np.float32)
    sin2 = np.concatenate([-sin, sin], axis=-1).astype(np.float32)
    return cos2, sin2


def _layer_weight(layer, rows, cols, buffers=2):
    return pl.BlockSpec((None, rows, cols), lambda t: (layer, 0, 0),
                        pipeline_mode=pl.Buffered(buffers))


def _in_proj_body(x_ref, g1_ref, w_ref, cos_ref, sin_ref, lng_ref, lnb_ref, ws_ref, bs_ref,
                  ygm_ref, q_ref, k_ref, v_ref, sg_ref, hglu_ref):
    h = _rmsnorm(x_ref[...], g1_ref[...]).astype(BF16)

    def proj(lo, hi):
        return jnp.dot(h, w_ref[:, lo:hi], preferred_element_type=F32)

    uv = proj(COL_GM, COL_Q)
    u = _gelu(uv[:, :GM_WIDTH])
    v = _gelu(uv[:, GM_WIDTH:])
    vln = (_standardize(v) * lng_ref[...] + lnb_ref[...]).astype(BF16)
    lane_head = lax.broadcasted_iota(jnp.int32, (CHUNK, GM_WIDTH), 1) // GM_HEAD_DIM
    for c in range(PROJ_TILE // CHUNK):
        rows = slice(c * CHUNK, (c + 1) * CHUNK)
        full = jnp.dot(ws_ref[...], vln[rows], preferred_element_type=F32)
        mixed = full[0:CHUNK]
        for hd in range(1, GM_HEADS):
            mixed = jnp.where(lane_head == hd, full[hd * CHUNK:(hd + 1) * CHUNK], mixed)
        ygm_ref[rows, :] = (u[rows] * (mixed + bs_ref[...])).astype(BF16)

    cos2 = cos_ref[...]
    sin2 = sin_ref[...]
    qk = proj(COL_Q, COL_V)
    scale = np.float32(RET_HEAD_DIM ** -0.5)
    for hd in range(RET_HEADS):
        cols = slice(hd * RET_HEAD_DIM, (hd + 1) * RET_HEAD_DIM)
        t = qk[:, cols]
        q_ref[:, cols] = (t * cos2 + pltpu.roll(t, RET_HEAD_DIM // 2, 1) * sin2).astype(BF16)
        t = qk[:, RET_WIDTH + hd * RET_HEAD_DIM:RET_WIDTH + (hd + 1) * RET_HEAD_DIM]
        k_ref[:, cols] = ((t * cos2 + pltpu.roll(t, RET_HEAD_DIM // 2, 1) * sin2) * scale).astype(BF16)
    vg = proj(COL_V, COL_CONV)
    v_ref[...] = vg[:, :RET_WIDTH].astype(BF16)
    g = vg[:, RET_WIDTH:]
    sg_ref[...] = (g * _sigmoid(g)).astype(BF16)

    ag = proj(COL_CONV, IN_WIDTH)
    hglu_ref[...] = ag[:, :CONV_WIDTH] * _sigmoid(ag[:, CONV_WIDTH:])


def _in_proj(x2, g1, w_in, layer, cos2, sin2, lng, lnb, ws, bs, seq):
    m = x2.shape[0]
    tiles_per_seq = seq // PROJ_TILE
    tok = lambda width: pl.BlockSpec((PROJ_TILE, width), lambda t: (t, 0))
    const = lambda shape: pl.BlockSpec(shape, lambda t: (0,) * len(shape))
    return pl.pallas_call(
        _in_proj_body,
        grid=(m // PROJ_TILE,),
        in_specs=[
            tok(D_MODEL), const((1, D_MODEL)), _layer_weight(layer, D_MODEL, IN_WIDTH),
            pl.BlockSpec((PROJ_TILE, RET_HEAD_DIM), lambda t: (t % tiles_per_seq, 0)),
            pl.BlockSpec((PROJ_TILE, RET_HEAD_DIM), lambda t: (t % tiles_per_seq, 0)),
            const((1, GM_WIDTH)), const((1, GM_WIDTH)),
            const((GM_HEADS * CHUNK, CHUNK)), const((CHUNK, GM_WIDTH)),
        ],
        out_specs=[tok(GM_WIDTH), tok(RET_WIDTH), tok(RET_WIDTH), tok(RET_WIDTH), tok(RET_WIDTH),
                   tok(CONV_WIDTH)],
        out_shape=[
            jax.ShapeDtypeStruct((m, GM_WIDTH), BF16),
            jax.ShapeDtypeStruct((m, RET_WIDTH), BF16),
            jax.ShapeDtypeStruct((m, RET_WIDTH), BF16),
            jax.ShapeDtypeStruct((m, RET_WIDTH), BF16),
            jax.ShapeDtypeStruct((m, RET_WIDTH), BF16),
            jax.ShapeDtypeStruct((m, CONV_WIDTH), F32),
        ],
        compiler_params=pltpu.CompilerParams(
            dimension_semantics=("parallel",), vmem_limit_bytes=VMEM_LIMIT_BYTES),
        name="in_proj",
    )(x2, g1, w_in, cos2, sin2, lng, lnb, ws, bs)


def _ret_state_body(gc_f, gc_b, n_blocks, kf_ref, vf_ref, kb_ref, vb_ref, zf_ref, zb_ref,
                    hmain_ref, hprev_ref, hnext_ref, cw_ref, cb_ref, clg_ref, clb_ref,
                    sf_ref, sb_ref, ycv_ref, st_ref, hpad_ref):
    i = pl.program_id(1)

    @pl.when(i == 0)
    def _():
        st_ref[...] = jnp.zeros_like(st_ref)

    rows_per_step = STATE_CHUNKS * CHUNK
    zero_halo = jnp.zeros((HALO_ROWS, CONV_WIDTH), F32)
    hpad_ref[0:HALO_ROWS, :] = jnp.where(i == 0, zero_halo, hprev_ref[...])
    hpad_ref[HALO_ROWS:HALO_ROWS + rows_per_step, :] = hmain_ref[...]
    hpad_ref[HALO_ROWS + rows_per_step:, :] = jnp.where(i == n_blocks - 1, zero_halo, hnext_ref[...])
    span = CONV_ROW_BLOCK + SUBLANES
    for rb in range(rows_per_step // CONV_ROW_BLOCK):
        r0 = rb * CONV_ROW_BLOCK
        halves = []
        for ch in range(CONV_WIDTH // LANES):
            cols = slice(ch * LANES, (ch + 1) * LANES)
            acc = None
            for r in range(SUBLANES):
                part = None
                for j in range(CONV_KERNEL):
                    if (j - CONV_PAD) % SUBLANES != r:
                        continue
                    start = r0 + HALO_ROWS + (j - CONV_PAD - r)
                    term = cw_ref[j:j + 1, cols] * hpad_ref[start:start + span, cols]
                    part = term if part is None else part + term
                if r:
                    part = pltpu.roll(part, span - r, 0)
                part = part[:CONV_ROW_BLOCK]
                acc = part if acc is None else acc + part
            halves.append(acc)
        acc = jnp.concatenate(halves, axis=1)
        hc = _standardize(acc + cb_ref[...]) * clg_ref[...] + clb_ref[...]
        ycv_ref[r0:r0 + CONV_ROW_BLOCK, :] = (hc * _sigmoid(hc)).astype(BF16)

    def chunk_kv(k_ref, v_ref, z_ref, j):
        rows = slice(j * CHUNK, (j + 1) * CHUNK)
        vz = (v_ref[rows, :].astype(F32) * z_ref[...]).astype(BF16)
        out = []
        for hd in range(RET_HEADS):
            cols = slice(hd * RET_HEAD_DIM, (hd + 1) * RET_HEAD_DIM)
            out.append(lax.dot_general(k_ref[rows, cols], vz[:, cols], (((0,), (0,)), ((), ())),
                                       preferred_element_type=F32))
        return out

    for d, (k_ref, v_ref, z_ref, s_ref, gc, order) in enumerate((
            (kf_ref, vf_ref, zf_ref, sf_ref, gc_f, range(STATE_CHUNKS)),
            (kb_ref, vb_ref, zb_ref, sb_ref, gc_b, range(STATE_CHUNKS - 1, -1, -1)))):
        for j in order:
            kv = chunk_kv(k_ref, v_ref, z_ref, j)
            for hd in range(RET_HEADS):
                state = st_ref[d, hd]
                s_ref[j, hd] = state.astype(BF16)
                st_ref[d, hd] = gc[hd] * state + kv[hd]


def _ret_state(k, v, zf, zb, gc_f, gc_b, hglu, cw, cb, clg, clb, batch, seq):
    rows = STATE_CHUNKS * CHUNK
    nb = seq // rows
    m = batch * seq
    n_chunks = m // CHUNK
    halo_per_step = rows // HALO_ROWS
    n_halo = m // HALO_ROWS
    step = lambda b, i: b * nb + i
    fwd = pl.BlockSpec((rows, RET_WIDTH), lambda b, i: (step(b, i), 0))
    bwd = pl.BlockSpec((rows, RET_WIDTH), lambda b, i: (b * nb + nb - 1 - i, 0))
    const = lambda shape: pl.BlockSpec(shape, lambda b, i: (0,) * len(shape))
    state_shape = (STATE_CHUNKS, RET_HEADS, RET_HEAD_DIM, RET_HEAD_DIM)
    return pl.pallas_call(
        functools.partial(_ret_state_body, gc_f, gc_b, nb),
        grid=(batch, nb),
        in_specs=[
            fwd, fwd, bwd, bwd, const((CHUNK, RET_WIDTH)), const((CHUNK, RET_WIDTH)),
            pl.BlockSpec((rows, CONV_WIDTH), lambda b, i: (step(b, i), 0)),
            pl.BlockSpec((HALO_ROWS, CONV_WIDTH),
                         lambda b, i: (jnp.maximum(step(b, i) * halo_per_step - 1, 0), 0)),
            pl.BlockSpec((HALO_ROWS, CONV_WIDTH),
                         lambda b, i: (jnp.minimum((step(b, i) + 1) * halo_per_step, n_halo - 1), 0)),
            const((CONV_KERNEL, CONV_WIDTH)), const((1, CONV_WIDTH)), const((1, CONV_WIDTH)),
            const((1, CONV_WIDTH)),
        ],
        out_specs=[
            pl.BlockSpec(state_shape, lambda b, i: (step(b, i), 0, 0, 0)),
            pl.BlockSpec(state_shape, lambda b, i: (b * nb + nb - 1 - i, 0, 0, 0)),
            pl.BlockSpec((rows, CONV_WIDTH), lambda b, i: (step(b, i), 0)),
        ],
        out_shape=[jax.ShapeDtypeStruct((n_chunks, RET_HEADS, RET_HEAD_DIM, RET_HEAD_DIM), BF16)] * 2
        + [jax.ShapeDtypeStruct((m, CONV_WIDTH), BF16)],
        scratch_shapes=[pltpu.VMEM((2, RET_HEADS, RET_HEAD_DIM, RET_HEAD_DIM), F32),
                        pltpu.VMEM((rows + 2 * HALO_ROWS, CONV_WIDTH), F32)],
        compiler_params=pltpu.CompilerParams(
            dimension_semantics=("parallel", "arbitrary"), vmem_limit_bytes=VMEM_LIMIT_BYTES),
        name="ret_state",
    )(k, v, k, v, zf, zb, hglu, hglu, hglu, cw, cb, clg, clb)


def _mix_ffn_body(final, x_ref, ygm_ref, q_ref, k_ref, v_ref, sg_ref, ycv_ref, sf_ref, sb_ref,
                  dmask_ref, xif_ref, xib_ref, wo_ref, g2_ref, win_ref, wdn_ref, gf_ref,
                  o_ref, p_ref, oret_ref, yret_ref, x1_ref):
    pairs = [(c, hd) for c in range(TOKEN_TILE // CHUNK) for hd in range(RET_HEADS)]

    def block(c, hd):
        return slice(c * CHUNK, (c + 1) * CHUNK), slice(hd * RET_HEAD_DIM, (hd + 1) * RET_HEAD_DIM)

    def masked_scores(i):
        c, hd = pairs[i]
        rows, cols = block(c, hd)
        scores = lax.dot_general(q_ref[rows, cols], k_ref[rows, cols], (((1,), (1,)), ((), ())),
                                 preferred_element_type=F32)
        p_ref[i] = (scores * dmask_ref[hd]).astype(BF16)

    def weighted_values(i):
        c, hd = pairs[i]
        rows, cols = block(c, hd)
        qh = q_ref[rows, cols]
        o = jnp.dot(p_ref[i], v_ref[rows, cols], preferred_element_type=F32)
        states = jnp.concatenate([sf_ref[c, hd], sb_ref[c, hd]], axis=1)
        cross = jnp.dot(qh, states, preferred_element_type=F32)
        o = o + xif_ref[:, cols] * cross[:, :RET_HEAD_DIM] + xib_ref[:, cols] * cross[:, RET_HEAD_DIM:]
        oret_ref[rows, cols] = o

    def head_norm_gate(hd):
        cols = slice(hd * RET_HEAD_DIM, (hd + 1) * RET_HEAD_DIM)
        yret_ref[:, cols] = (_standardize(oret_ref[:, cols])
                             * sg_ref[:, cols].astype(F32)).astype(BF16)

    for i in range(len(pairs)):
        masked_scores(i)
    for i in range(len(pairs)):
        weighted_values(i)
    for hd in range(RET_HEADS):
        head_norm_gate(hd)
    x1_ref[...] = (x_ref[...]
                   + jnp.dot(ygm_ref[...], wo_ref[0:GM_WIDTH, :], preferred_element_type=F32)
                   + jnp.dot(yret_ref[...], wo_ref[GM_WIDTH:GM_WIDTH + RET_WIDTH, :],
                             preferred_element_type=F32)
                   + jnp.dot(ycv_ref[...], wo_ref[GM_WIDTH + RET_WIDTH:, :],
                             preferred_element_type=F32))

    h = _rmsnorm(x1_ref[...], g2_ref[...]).astype(BF16)
    acc = None
    for lo in range(0, FFN_HIDDEN, FFN_COL_BLOCK):
        hi = min(lo + FFN_COL_BLOCK, FFN_HIDDEN)
        gate = jnp.dot(h, win_ref[:, lo:hi], preferred_element_type=F32)
        up = jnp.dot(h, win_ref[:, FFN_HIDDEN + lo:FFN_HIDDEN + hi], preferred_element_type=F32)
        act = (gate * _sigmoid(gate) * up).astype(BF16)
        part = jnp.dot(act, wdn_ref[lo:hi, :], preferred_element_type=F32)
        acc = part if acc is None else acc + part
    acc = x1_ref[...] + acc
    o_ref[...] = _rmsnorm(acc, gf_ref[...]) if final else acc


def _mix_ffn(x2, ygm, q, k, v, sg, ycv, sf, sb, dmask, xif, xib, wo, g2, win, wdn, layer, gf, final):
    m = x2.shape[0]
    tok = lambda width: pl.BlockSpec((TOKEN_TILE, width), lambda t: (t, 0))
    const = lambda shape: pl.BlockSpec(shape, lambda t: (0,) * len(shape))
    state = pl.BlockSpec((TOKEN_TILE // CHUNK, RET_HEADS, RET_HEAD_DIM, RET_HEAD_DIM),
                         lambda t: (t, 0, 0, 0))
    return pl.pallas_call(
        functools.partial(_mix_ffn_body, final),
        grid=(m // TOKEN_TILE,),
        in_specs=[
            tok(D_MODEL), tok(GM_WIDTH), tok(RET_WIDTH), tok(RET_WIDTH), tok(RET_WIDTH), tok(RET_WIDTH),
            tok(CONV_WIDTH), state, state,
            const((RET_HEADS, CHUNK, CHUNK)), const((CHUNK, RET_WIDTH)), const((CHUNK, RET_WIDTH)),
            _layer_weight(layer, D_MODEL, D_MODEL, 1), const((1, D_MODEL)),
            _layer_weight(layer, D_MODEL, 2 * FFN_HIDDEN, 1), _layer_weight(layer, FFN_HIDDEN, D_MODEL, 1),
            const((1, D_MODEL)),
        ],
        out_specs=tok(D_MODEL),
        out_shape=jax.ShapeDtypeStruct((m, D_MODEL), F32),
        scratch_shapes=[pltpu.VMEM((TOKEN_TILE // CHUNK * RET_HEADS, CHUNK, CHUNK), BF16),
                        pltpu.VMEM((TOKEN_TILE, RET_WIDTH), F32),
                        pltpu.VMEM((TOKEN_TILE, RET_WIDTH), BF16),
                        pltpu.VMEM((TOKEN_TILE, D_MODEL), F32)],
        compiler_params=pltpu.CompilerParams(
            dimension_semantics=("parallel",), vmem_limit_bytes=VMEM_LIMIT_BYTES),
        name="mix_ffn_final" if final else "mix_ffn",
    )(x2, ygm, q, k, v, sg, ycv, sf, sb, dmask, xif, xib, wo, g2, win, wdn, gf)


def kernel(x, norm1_g, w_in, gm_ln_g, gm_ln_b, gm_ws, gm_bs, conv_w, conv_b, conv_ln_g, conv_ln_b,
           w_out, norm2_g, w_ffn_in, w_ffn_out, final_g):
    batch, seq, _ = x.shape
    depth = w_in.shape[0]
    assert all(seq % t == 0 for t in (STATE_CHUNKS * CHUNK, TOKEN_TILE, PROJ_TILE))

    dmask, zf, zb, xif, xib, gc_f, gc_b = _retention_tables()
    cos2, sin2 = _rotary_tables(seq)
    row = lambda a: a.reshape(1, -1)

    w_in_b, w_out_b = w_in.astype(BF16), w_out.astype(BF16)
    w_ffn_in_b, w_ffn_out_b = w_ffn_in.astype(BF16), w_ffn_out.astype(BF16)

    x2 = x.reshape(batch * seq, D_MODEL)
    for l in range(depth):
        ws = gm_ws[l].reshape(GM_HEADS * CHUNK, CHUNK).astype(BF16)
        bs = jnp.repeat(gm_bs[l].T, GM_HEAD_DIM, axis=1)
        ygm, q, k, v, sg, hglu = _in_proj(
            x2, row(norm1_g[l]), w_in_b, l, cos2, sin2,
            row(gm_ln_g[l]), row(gm_ln_b[l]), ws, bs, seq)
        sf, sb, ycv = _ret_state(k, v, zf, zb, gc_f, gc_b, hglu, conv_w[l], row(conv_b[l]),
                                 row(conv_ln_g[l]), row(conv_ln_b[l]), batch, seq)
        x2 = _mix_ffn(x2, ygm, q, k, v, sg, ycv, sf, sb, dmask, xif, xib, w_out_b,
                      row(norm2_g[l]), w_ffn_in_b, w_ffn_out_b, l, row(final_g),
                      final=(l == depth - 1))
    return x2.reshape(batch, seq, D_MODEL)
```

```python
import functools

import numpy as np
import jax
import jax.numpy as jnp
from jax import lax
from jax.experimental import pallas as pl
from jax.experimental.pallas import tpu as pltpu

F32 = jnp.float32
BF16 = jnp.bfloat16

D_MODEL = 1024
GM_WIDTH = 256
GM_HEADS = 4
GM_HEAD_DIM = GM_WIDTH // GM_HEADS
CHUNK = 128
RET_WIDTH = 512
RET_HEADS = 4
RET_HEAD_DIM = RET_WIDTH // RET_HEADS
CONV_WIDTH = 256
CONV_KERNEL = 31
CONV_PAD = CONV_KERNEL // 2
IN_WIDTH = 2 * GM_WIDTH + 4 * RET_WIDTH + 2 * CONV_WIDTH
FFN_HIDDEN = 2816
ROPE_BASE = 10000.0
EPS = 1e-6

COL_GM = 0
COL_Q = 2 * GM_WIDTH
COL_V = COL_Q + 2 * RET_WIDTH
COL_CONV = COL_V + 2 * RET_WIDTH

TOKEN_TILE = 512
PROJ_TILE = 1024
HALO_ROWS = 16
STATE_CHUNKS = 16
CONV_ROW_BLOCK = 128
SUBLANES = 8
LANES = 128
FFN_COL_BLOCK = 512
VMEM_LIMIT_BYTES = 56 * 1024 * 1024


def _gelu(x):
    return 0.5 * x * (1.0 + lax.erf(x * np.float32(1.0 / np.sqrt(2.0))))


def _sigmoid(x):
    return 1.0 / (1.0 + jnp.exp(-x))


def _standardize(x):
    mu = jnp.mean(x, axis=-1, keepdims=True)
    xc = x - mu
    var = jnp.mean(xc * xc, axis=-1, keepdims=True)
    return xc * lax.rsqrt(var + EPS)


def _rms_factor(x):
    return lax.rsqrt(jnp.mean(x * x, axis=-1, keepdims=True) + EPS)


def _rmsnorm(x, g):
    return x * _rms_factor(x) * g


def _retention_tables():
    idx = np.arange(CHUNK, dtype=np.float32)
    gamma_f = (1.0 - np.exp2(-5.0 - np.arange(RET_HEADS, dtype=np.float32))).astype(np.float32)
    gamma_b = gamma_f[::-1]
    lf = np.log(gamma_f)[:, None]
    lb = np.log(gamma_b)[:, None]
    diff = idx[:, None] - idx[None, :]
    dmask = np.where(diff >= 0,
                     np.exp(lf[:, :, None] * np.maximum(diff, 0.0)),
                     np.exp(lb[:, :, None] * np.maximum(-diff, 0.0))).astype(np.float32)
    zeta_f = np.exp(lf * (CHUNK - 1 - idx))
    zeta_b = np.exp(lb * idx)
    xi_f = np.exp(lf * (idx + 1))
    xi_b = np.exp(lb * (CHUNK - idx))

    def per_row(t):
        return np.repeat(t.T.astype(np.float32), RET_HEAD_DIM, axis=1)

    gc_f = [float(np.exp(np.float32(l) * np.float32(CHUNK))) for l in lf[:, 0]]
    gc_b = [float(np.exp(np.float32(l) * np.float32(CHUNK))) for l in lb[:, 0]]
    return dmask, per_row(zeta_f), per_row(zeta_b), per_row(xi_f), per_row(xi_b), gc_f, gc_b


def _rotary_tables(seq):
    half = RET_HEAD_DIM // 2
    inv_freq = ROPE_BASE ** (-np.arange(half, dtype=np.float64) / half)
    ang = np.arange(seq, dtype=np.float64)[:, None] * inv_freq[None, :]
    cos, sin = np.cos(ang), np.sin(ang)
    cos2 = np.concatenate([cos, cos], axis=-1).astype(np.float32)
    sin2 = np.concatenate([-sin, sin], axis=-1).astype(np.float32)
    return cos2, sin2


def _layer_weight(layer, rows, cols, buffers=2):
    return pl.BlockSpec((None, rows, cols), lambda t: (layer, 0, 0),
                        pipeline_mode=pl.Buffered(buffers))


def _in_proj_body(x_ref, g1_ref, w_ref, cos_ref, sin_ref, lng_ref, lnb_ref, ws_ref, bs_ref,
                  ygm_ref, q_ref, k_ref, v_ref, sg_ref, hglu_ref):
    h = _rmsnorm(x_ref[...], g1_ref[...]).astype(BF16)

    def proj(lo, hi):
        return jnp.dot(h, w_ref[:, lo:hi], preferred_element_type=F32)

    uv = proj(COL_GM, COL_Q)
    u = _gelu(uv[:, :GM_WIDTH])
    v = _gelu(uv[:, GM_WIDTH:])
    vln = (_standardize(v) * lng_ref[...] + lnb_ref[...]).astype(BF16)
    lane_head = lax.broadcasted_iota(jnp.int32, (CHUNK, GM_WIDTH), 1) // GM_HEAD_DIM
    for c in range(PROJ_TILE // CHUNK):
        rows = slice(c * CHUNK, (c + 1) * CHUNK)
        full = jnp.dot(ws_ref[...], vln[rows], preferred_element_type=F32)
        mixed = full[0:CHUNK]
        for hd in range(1, GM_HEADS):
            mixed = jnp.where(lane_head == hd, full[hd * CHUNK:(hd + 1) * CHUNK], mixed)
        ygm_ref[rows, :] = (u[rows] * (mixed + bs_ref[...])).astype(BF16)

    cos2 = cos_ref[...]
    sin2 = sin_ref[...]
    qk = proj(COL_Q, COL_V)
    scale = np.float32(RET_HEAD_DIM ** -0.5)
    for hd in range(RET_HEADS):
        cols = slice(hd * RET_HEAD_DIM, (hd + 1) * RET_HEAD_DIM)
        t = qk[:, cols]
        q_ref[:, cols] = (t * cos2 + pltpu.roll(t, RET_HEAD_DIM // 2, 1) * sin2).astype(BF16)
        t = qk[:, RET_WIDTH + hd * RET_HEAD_DIM:RET_WIDTH + (hd + 1) * RET_HEAD_DIM]
        k_ref[:, cols] = ((t * cos2 + pltpu.roll(t, RET_HEAD_DIM // 2, 1) * sin2) * scale).astype(BF16)
    vg = proj(COL_V, COL_CONV)
    v_ref[...] = vg[:, :RET_WIDTH].astype(BF16)
    g = vg[:, RET_WIDTH:]
    sg_ref[...] = (g * _sigmoid(g)).astype(BF16)

    ag = proj(COL_CONV, IN_WIDTH)
    hglu_ref[...] = ag[:, :CONV_WIDTH] * _sigmoid(ag[:, CONV_WIDTH:])


def _in_proj(x2, g1, w_in, layer, cos2, sin2, lng, lnb, ws, bs, seq):
    m = x2.shape[0]
    tiles_per_seq = seq // PROJ_TILE
    tok = lambda width: pl.BlockSpec((PROJ_TILE, width), lambda t: (t, 0))
    const = lambda shape: pl.BlockSpec(shape, lambda t: (0,) * len(shape))
    return pl.pallas_call(
        _in_proj_body,
        grid=(m // PROJ_TILE,),
        in_specs=[
            tok(D_MODEL), const((1, D_MODEL)), _layer_weight(layer, D_MODEL, IN_WIDTH),
            pl.BlockSpec((PROJ_TILE, RET_HEAD_DIM), lambda t: (t % tiles_per_seq, 0)),
            pl.BlockSpec((PROJ_TILE, RET_HEAD_DIM), lambda t: (t % tiles_per_seq, 0)),
            const((1, GM_WIDTH)), const((1, GM_WIDTH)),
            const((GM_HEADS * CHUNK, CHUNK)), const((CHUNK, GM_WIDTH)),
        ],
        out_specs=[tok(GM_WIDTH), tok(RET_WIDTH), tok(RET_WIDTH), tok(RET_WIDTH), tok(RET_WIDTH),
                   tok(CONV_WIDTH)],
        out_shape=[
            jax.ShapeDtypeStruct((m, GM_WIDTH), BF16),
            jax.ShapeDtypeStruct((m, RET_WIDTH), BF16),
            jax.ShapeDtypeStruct((m, RET_WIDTH), BF16),
            jax.ShapeDtypeStruct((m, RET_WIDTH), BF16),
            jax.ShapeDtypeStruct((m, RET_WIDTH), BF16),
            jax.ShapeDtypeStruct((m, CONV_WIDTH), F32),
        ],
        compiler_params=pltpu.CompilerParams(
            dimension_semantics=("parallel",), vmem_limit_bytes=VMEM_LIMIT_BYTES),
        name="in_proj",
    )(x2, g1, w_in, cos2, sin2, lng, lnb, ws, bs)


def _ret_state_body(gc_f, gc_b, n_blocks, kf_ref, vf_ref, kb_ref, vb_ref, zf_ref, zb_ref,
                    hmain_ref, hprev_ref, hnext_ref, cw_ref, cb_ref, clg_ref, clb_ref,
                    sf_ref, sb_ref, ycv_ref, st_ref, hpad_ref):
    i = pl.program_id(1)

    @pl.when(i == 0)
    def _():
        st_ref[...] = jnp.zeros_like(st_ref)

    rows_per_step = STATE_CHUNKS * CHUNK
    zero_halo = jnp.zeros((HALO_ROWS, CONV_WIDTH), F32)
    hpad_ref[0:HALO_ROWS, :] = jnp.where(i == 0, zero_halo, hprev_ref[...])
    hpad_ref[HALO_ROWS:HALO_ROWS + rows_per_step, :] = hmain_ref[...]
    hpad_ref[HALO_ROWS + rows_per_step:, :] = jnp.where(i == n_blocks - 1, zero_halo, hnext_ref[...])
    span = CONV_ROW_BLOCK + SUBLANES
    for rb in range(rows_per_step // CONV_ROW_BLOCK):
        r0 = rb * CONV_ROW_BLOCK
        halves = []
        for ch in range(CONV_WIDTH // LANES):
            cols = slice(ch * LANES, (ch + 1) * LANES)
            acc = None
            for r in range(SUBLANES):
                part = None
                for j in range(CONV_KERNEL):
                    if (j - CONV_PAD) % SUBLANES != r:
                        continue
                    start = r0 + HALO_ROWS + (j - CONV_PAD - r)
                    term = cw_ref[j:j + 1, cols] * hpad_ref[start:start + span, cols]
                    part = term if part is None else part + term
                if r:
                    part = pltpu.roll(part, span - r, 0)
                part = part[:CONV_ROW_BLOCK]
                acc = part if acc is None else acc + part
            halves.append(acc)
        acc = jnp.concatenate(halves, axis=1)
        hc = _standardize(acc + cb_ref[...]) * clg_ref[...] + clb_ref[...]
        ycv_ref[r0:r0 + CONV_ROW_BLOCK, :] = (hc * _sigmoid(hc)).astype(BF16)

    def chunk_kv(k_ref, v_ref, z_ref, j):
        rows = slice(j * CHUNK, (j + 1) * CHUNK)
        vz = (v_ref[rows, :].astype(F32) * z_ref[...]).astype(BF16)
        out = []
        for hd in range(RET_HEADS):
            cols = slice(hd * RET_HEAD_DIM, (hd + 1) * RET_HEAD_DIM)
            out.append(lax.dot_general(k_ref[rows, cols], vz[:, cols], (((0,), (0,)), ((), ())),
                                       preferred_element_type=F32))
        return out

    for d, (k_ref, v_ref, z_ref, s_ref, gc, order) in enumerate((
            (kf_ref, vf_ref, zf_ref, sf_ref, gc_f, range(STATE_CHUNKS)),
            (kb_ref, vb_ref, zb_ref, sb_ref, gc_b, range(STATE_CHUNKS - 1, -1, -1)))):
        for j in order:
            kv = chunk_kv(k_ref, v_ref, z_ref, j)
            for hd in range(RET_HEADS):
                state = st_ref[d, hd]
                s_ref[j, hd] = state.astype(BF16)
                st_ref[d, hd] = gc[hd] * state + kv[hd]


def _ret_state(k, v, zf, zb, gc_f, gc_b, hglu, cw, cb, clg, clb, batch, seq):
    rows = STATE_CHUNKS * CHUNK
    nb = seq // rows
    m = batch * seq
    n_chunks = m // CHUNK
    halo_per_step = rows // HALO_ROWS
    n_halo = m // HALO_ROWS
    step = lambda b, i: b * nb + i
    fwd = pl.BlockSpec((rows, RET_WIDTH), lambda b, i: (step(b, i), 0))
    bwd = pl.BlockSpec((rows, RET_WIDTH), lambda b, i: (b * nb + nb - 1 - i, 0))
    const = lambda shape: pl.BlockSpec(shape, lambda b, i: (0,) * len(shape))
    state_shape = (STATE_CHUNKS, RET_HEADS, RET_HEAD_DIM, RET_HEAD_DIM)
    return pl.pallas_call(
        functools.partial(_ret_state_body, gc_f, gc_b, nb),
        grid=(batch, nb),
        in_specs=[
            fwd, fwd, bwd, bwd, const((CHUNK, RET_WIDTH)), const((CHUNK, RET_WIDTH)),
            pl.BlockSpec((rows, CONV_WIDTH), lambda b, i: (step(b, i), 0)),
            pl.BlockSpec((HALO_ROWS, CONV_WIDTH),
                         lambda b, i: (jnp.maximum(step(b, i) * halo_per_step - 1, 0), 0)),
            pl.BlockSpec((HALO_ROWS, CONV_WIDTH),
                         lambda b, i: (jnp.minimum((step(b, i) + 1) * halo_per_step, n_halo - 1), 0)),
            const((CONV_KERNEL, CONV_WIDTH)), const((1, CONV_WIDTH)), const((1, CONV_WIDTH)),
            const((1, CONV_WIDTH)),
        ],
        out_specs=[
            pl.BlockSpec(state_shape, lambda b, i: (step(b, i), 0, 0, 0)),
            pl.BlockSpec(state_shape, lambda b, i: (b * nb + nb - 1 - i, 0, 0, 0)),
            pl.BlockSpec((rows, CONV_WIDTH), lambda b, i: (step(b, i), 0)),
        ],
        out_shape=[jax.ShapeDtypeStruct((n_chunks, RET_HEADS, RET_HEAD_DIM, RET_HEAD_DIM), BF16)] * 2
        + [jax.ShapeDtypeStruct((m, CONV_WIDTH), BF16)],
        scratch_shapes=[pltpu.VMEM((2, RET_HEADS, RET_HEAD_DIM, RET_HEAD_DIM), F32),
                        pltpu.VMEM((rows + 2 * HALO_ROWS, CONV_WIDTH), F32)],
        compiler_params=pltpu.CompilerParams(
            dimension_semantics=("parallel", "arbitrary"), vmem_limit_bytes=VMEM_LIMIT_BYTES),
        name="ret_state",
    )(k, v, k, v, zf, zb, hglu, hglu, hglu, cw, cb, clg, clb)


def _mix_ffn_body(final, x_ref, ygm_ref, q_ref, k_ref, v_ref, sg_ref, ycv_ref, sf_ref, sb_ref,
                  dmask_ref, xif_ref, xib_ref, wo_ref, g2_ref, win_ref, wdn_ref, gf_ref,
                  o_ref, p_ref, oret_ref, yret_ref, x1_ref):
    pairs = [(c, hd) for c in range(TOKEN_TILE // CHUNK) for hd in range(RET_HEADS)]

    def block(c, hd):
        return slice(c * CHUNK, (c + 1) * CHUNK), slice(hd * RET_HEAD_DIM, (hd + 1) * RET_HEAD_DIM)

    def masked_scores(i):
        c, hd = pairs[i]
        rows, cols = block(c, hd)
        scores = lax.dot_general(q_ref[rows, cols], k_ref[rows, cols], (((1,), (1,)), ((), ())),
                                 preferred_element_type=F32)
        p_ref[i] = (scores * dmask_ref[hd]).astype(BF16)

    def weighted_values(i):
        c, hd = pairs[i]
        rows, cols = block(c, hd)
        qh = q_ref[rows, cols]
        o = jnp.dot(p_ref[i], v_ref[rows, cols], preferred_element_type=F32)
        states = jnp.concatenate([sf_ref[c, hd], sb_ref[c, hd]], axis=1)
        cross = jnp.dot(qh, states, preferred_element_type=F32)
        o = o + xif_ref[:, cols] * cross[:, :RET_HEAD_DIM] + xib_ref[:, cols] * cross[:, RET_HEAD_DIM:]
        oret_ref[rows, cols] = o

    def head_norm_gate(hd):
        cols = slice(hd * RET_HEAD_DIM, (hd + 1) * RET_HEAD_DIM)
        yret_ref[:, cols] = (_standardize(oret_ref[:, cols])
                             * sg_ref[:, cols].astype(F32)).astype(BF16)

    for i in range(len(pairs)):
        masked_scores(i)
    for i in range(len(pairs)):
        weighted_values(i)
    for hd in range(RET_HEADS):
        head_norm_gate(hd)
    x1_ref[...] = (x_ref[...]
                   + jnp.dot(ygm_ref[...], wo_ref[0:GM_WIDTH, :], preferred_element_type=F32)
                   + jnp.dot(yret_ref[...], wo_ref[GM_WIDTH:GM_WIDTH + RET_WIDTH, :],
                             preferred_element_type=F32)
                   + jnp.dot(ycv_ref[...], wo_ref[GM_WIDTH + RET_WIDTH:, :],
                             preferred_element_type=F32))

    x1 = x1_ref[...]
    rms = _rms_factor(x1)
    h = (x1 * g2_ref[...]).astype(BF16)
    acc = None
    for lo in range(0, FFN_HIDDEN, FFN_COL_BLOCK):
        hi = min(lo + FFN_COL_BLOCK, FFN_HIDDEN)
        gate = rms * jnp.dot(h, win_ref[:, lo:hi], preferred_element_type=F32)
        up = rms * jnp.dot(h, win_ref[:, FFN_HIDDEN + lo:FFN_HIDDEN + hi],
                           preferred_element_type=F32)
        act = (gate * _sigmoid(gate) * up).astype(BF16)
        part = jnp.dot(act, wdn_ref[lo:hi, :], preferred_element_type=F32)
        acc = part if acc is None else acc + part
    acc = x1_ref[...] + acc
    o_ref[...] = _rmsnorm(acc, gf_ref[...]) if final else acc


def _mix_ffn(x2, ygm, q, k, v, sg, ycv, sf, sb, dmask, xif, xib, wo, g2, win, wdn, layer, gf, final):
    m = x2.shape[0]
    tok = lambda width: pl.BlockSpec((TOKEN_TILE, width), lambda t: (t, 0))
    const = lambda shape: pl.BlockSpec(shape, lambda t: (0,) * len(shape))
    state = pl.BlockSpec((TOKEN_TILE // CHUNK, RET_HEADS, RET_HEAD_DIM, RET_HEAD_DIM),
                         lambda t: (t, 0, 0, 0))
    return pl.pallas_call(
        functools.partial(_mix_ffn_body, final),
        grid=(m // TOKEN_TILE,),
        in_specs=[
            tok(D_MODEL), tok(GM_WIDTH), tok(RET_WIDTH), tok(RET_WIDTH), tok(RET_WIDTH), tok(RET_WIDTH),
            tok(CONV_WIDTH), state, state,
            const((RET_HEADS, CHUNK, CHUNK)), const((CHUNK, RET_WIDTH)), const((CHUNK, RET_WIDTH)),
            _layer_weight(layer, D_MODEL, D_MODEL, 1), const((1, D_MODEL)),
            _layer_weight(layer, D_MODEL, 2 * FFN_HIDDEN, 1), _layer_weight(layer, FFN_HIDDEN, D_MODEL, 1),
            const((1, D_MODEL)),
        ],
        out_specs=tok(D_MODEL),
        out_shape=jax.ShapeDtypeStruct((m, D_MODEL), F32),
        scratch_shapes=[pltpu.VMEM((TOKEN_TILE // CHUNK * RET_HEADS, CHUNK, CHUNK), BF16),
                        pltpu.VMEM((TOKEN_TILE, RET_WIDTH), F32),
                        pltpu.VMEM((TOKEN_TILE, RET_WIDTH), BF16),
                        pltpu.VMEM((TOKEN_TILE, D_MODEL), F32)],
        compiler_params=pltpu.CompilerParams(
            dimension_semantics=("parallel",), vmem_limit_bytes=VMEM_LIMIT_BYTES),
        name="mix_ffn_final" if final else "mix_ffn",
    )(x2, ygm, q, k, v, sg, ycv, sf, sb, dmask, xif, xib, wo, g2, win, wdn, gf)


def kernel(x, norm1_g, w_in, gm_ln_g, gm_ln_b, gm_ws, gm_bs, conv_w, conv_b, conv_ln_g, conv_ln_b,
           w_out, norm2_g, w_ffn_in, w_ffn_out, final_g):
    batch, seq, _ = x.shape
    depth = w_in.shape[0]
    assert all(seq % t == 0 for t in (STATE_CHUNKS * CHUNK, TOKEN_TILE, PROJ_TILE))

    dmask, zf, zb, xif, xib, gc_f, gc_b = _retention_tables()
    cos2, sin2 = _rotary_tables(seq)
    row = lambda a: a.reshape(1, -1)

    w_in_b, w_out_b = w_in.astype(BF16), w_out.astype(BF16)
    w_ffn_in_b, w_ffn_out_b = w_ffn_in.astype(BF16), w_ffn_out.astype(BF16)

    x2 = x.reshape(batch * seq, D_MODEL)
    for l in range(depth):
        ws = gm_ws[l].reshape(GM_HEADS * CHUNK, CHUNK).astype(BF16)
        bs = jnp.repeat(gm_bs[l].T, GM_HEAD_DIM, axis=1)
        ygm, q, k, v, sg, hglu = _in_proj(
            x2, row(norm1_g[l]), w_in_b, l, cos2, sin2,
            row(gm_ln_g[l]), row(gm_ln_b[l]), ws, bs, seq)
        sf, sb, ycv = _ret_state(k, v, zf, zb, gc_f, gc_b, hglu, conv_w[l], row(conv_b[l]),
                                 row(conv_ln_g[l]), row(conv_ln_b[l]), batch, seq)
        x2 = _mix_ffn(x2, ygm, q, k, v, sg, ycv, sf, sb, dmask, xif, xib, w_out_b,
                      row(norm2_g[l]), w_ffn_in_b, w_ffn_out_b, l, row(final_g),
                      final=(l == depth - 1))
    return x2.reshape(batch, seq, D_MODEL)
```

```python
import functools

import numpy as np
import jax
import jax.numpy as jnp
from jax import lax
from jax.experimental import pallas as pl
from jax.experimental.pallas import tpu as pltpu

F32 = jnp.float32
BF16 = jnp.bfloat16

D_MODEL = 1024
GM_WIDTH = 256
GM_HEADS = 4
GM_HEAD_DIM = GM_WIDTH // GM_HEADS
CHUNK = 128
RET_WIDTH = 512
RET_HEADS = 4
RET_HEAD_DIM = RET_WIDTH // RET_HEADS
CONV_WIDTH = 256
CONV_KERNEL = 31
CONV_PAD = CONV_KERNEL // 2
IN_WIDTH = 2 * GM_WIDTH + 4 * RET_WIDTH + 2 * CONV_WIDTH
FFN_HIDDEN = 2816
ROPE_BASE = 10000.0
EPS = 1e-6

COL_GM = 0
COL_Q = 2 * GM_WIDTH
COL_V = COL_Q + 2 * RET_WIDTH
COL_CONV = COL_V + 2 * RET_WIDTH

TOKEN_TILE = 512
PROJ_TILE = 1024
HALO_ROWS = 16
STATE_CHUNKS = 16
CONV_ROW_BLOCK = 256
SUBLANES = 8
LANES = 128
FFN_COL_BLOCK = 512
VMEM_LIMIT_BYTES = 56 * 1024 * 1024


def _gelu(x):
    return 0.5 * x * (1.0 + lax.erf(x * np.float32(1.0 / np.sqrt(2.0))))


def _sigmoid(x):
    return 1.0 / (1.0 + jnp.exp(-x))


def _standardize(x):
    mu = jnp.mean(x, axis=-1, keepdims=True)
    xc = x - mu
    var = jnp.mean(xc * xc, axis=-1, keepdims=True)
    return xc * lax.rsqrt(var + EPS)


def _rms_factor(x):
    return lax.rsqrt(jnp.mean(x * x, axis=-1, keepdims=True) + EPS)


def _rmsnorm(x, g):
    return x * _rms_factor(x) * g


def _retention_tables():
    idx = np.arange(CHUNK, dtype=np.float32)
    gamma_f = (1.0 - np.exp2(-5.0 - np.arange(RET_HEADS, dtype=np.float32))).astype(np.float32)
    gamma_b = gamma_f[::-1]
    lf = np.log(gamma_f)[:, None]
    lb = np.log(gamma_b)[:, None]
    diff = idx[:, None] - idx[None, :]
    dmask = np.where(diff >= 0,
                     np.exp(lf[:, :, None] * np.maximum(diff, 0.0)),
                     np.exp(lb[:, :, None] * np.maximum(-diff, 0.0))).astype(np.float32)
    zeta_f = np.exp(lf * (CHUNK - 1 - idx))
    zeta_b = np.exp(lb * idx)
    xi_f = np.exp(lf * (idx + 1))
    xi_b = np.exp(lb * (CHUNK - idx))

    def per_row(t):
        return np.repeat(t.T.astype(np.float32), RET_HEAD_DIM, axis=1)

    gc_f = [float(np.exp(np.float32(l) * np.float32(CHUNK))) for l in lf[:, 0]]
    gc_b = [float(np.exp(np.float32(l) * np.float32(CHUNK))) for l in lb[:, 0]]
    return dmask, per_row(zeta_f), per_row(zeta_b), per_row(xi_f), per_row(xi_b), gc_f, gc_b


def _rotary_tables(seq):
    half = RET_HEAD_DIM // 2
    inv_freq = ROPE_BASE ** (-np.arange(half, dtype=np.float64) / half)
    ang = np.arange(seq, dtype=np.float64)[:, None] * inv_freq[None, :]
    cos, sin = np.cos(ang), np.sin(ang)
    cos2 = np.concatenate([cos, cos], axis=-1).astype(np.float32)
    sin2 = np.concatenate([-sin, sin], axis=-1).astype(np.float32)
    return cos2, sin2


def _layer_weight(layer, rows, cols, buffers=2):
    return pl.BlockSpec((None, rows, cols), lambda t: (layer, 0, 0),
                        pipeline_mode=pl.Buffered(buffers))


def _in_proj_body(x_ref, g1_ref, w_ref, cos_ref, sin_ref, lng_ref, lnb_ref, ws_ref, bs_ref,
                  ygm_ref, q_ref, k_ref, v_ref, sg_ref, hglu_ref):
    h = _rmsnorm(x_ref[...], g1_ref[...]).astype(BF16)

    def proj(lo, hi):
        return jnp.dot(h, w_ref[:, lo:hi], preferred_element_type=F32)

    uv = proj(COL_GM, COL_Q)
    u = _gelu(uv[:, :GM_WIDTH])
    v = _gelu(uv[:, GM_WIDTH:])
    vln = (_standardize(v) * lng_ref[...] + lnb_ref[...]).astype(BF16)
    lane_head = lax.broadcasted_iota(jnp.int32, (CHUNK, GM_WIDTH), 1) // GM_HEAD_DIM
    for c in range(PROJ_TILE // CHUNK):
        rows = slice(c * CHUNK, (c + 1) * CHUNK)
        full = jnp.dot(ws_ref[...], vln[rows], preferred_element_type=F32)
        mixed = full[0:CHUNK]
        for hd in range(1, GM_HEADS):
            mixed = jnp.where(lane_head == hd, full[hd * CHUNK:(hd + 1) * CHUNK], mixed)
        ygm_ref[rows, :] = (u[rows] * (mixed + bs_ref[...])).astype(BF16)

    cos2 = cos_ref[...]
    sin2 = sin_ref[...]
    qk = proj(COL_Q, COL_V)
    scale = np.float32(RET_HEAD_DIM ** -0.5)
    for hd in range(RET_HEADS):
        cols = slice(hd * RET_HEAD_DIM, (hd + 1) * RET_HEAD_DIM)
        t = qk[:, cols]
        q_ref[:, cols] = (t * cos2 + pltpu.roll(t, RET_HEAD_DIM // 2, 1) * sin2).astype(BF16)
        t = qk[:, RET_WIDTH + hd * RET_HEAD_DIM:RET_WIDTH + (hd + 1) * RET_HEAD_DIM]
        k_ref[:, cols] = ((t * cos2 + pltpu.roll(t, RET_HEAD_DIM // 2, 1) * sin2) * scale).astype(BF16)
    vg = proj(COL_V, COL_CONV)
    v_ref[...] = vg[:, :RET_WIDTH].astype(BF16)
    g = vg[:, RET_WIDTH:]
    sg_ref[...] = (g * _sigmoid(g)).astype(BF16)

    ag = proj(COL_CONV, IN_WIDTH)
    hglu_ref[...] = ag[:, :CONV_WIDTH] * _sigmoid(ag[:, CONV_WIDTH:])


def _in_proj(x2, g1, w_in, layer, cos2, sin2, lng, lnb, ws, bs, seq):
    m = x2.shape[0]
    tiles_per_seq = seq // PROJ_TILE
    tok = lambda width: pl.BlockSpec((PROJ_TILE, width), lambda t: (t, 0))
    const = lambda shape: pl.BlockSpec(shape, lambda t: (0,) * len(shape))
    return pl.pallas_call(
        _in_proj_body,
        grid=(m // PROJ_TILE,),
        in_specs=[
            tok(D_MODEL), const((1, D_MODEL)), _layer_weight(layer, D_MODEL, IN_WIDTH),
            pl.BlockSpec((PROJ_TILE, RET_HEAD_DIM), lambda t: (t % tiles_per_seq, 0)),
            pl.BlockSpec((PROJ_TILE, RET_HEAD_DIM), lambda t: (t % tiles_per_seq, 0)),
            const((1, GM_WIDTH)), const((1, GM_WIDTH)),
            const((GM_HEADS * CHUNK, CHUNK)), const((CHUNK, GM_WIDTH)),
        ],
        out_specs=[tok(GM_WIDTH), tok(RET_WIDTH), tok(RET_WIDTH), tok(RET_WIDTH), tok(RET_WIDTH),
                   tok(CONV_WIDTH)],
        out_shape=[
            jax.ShapeDtypeStruct((m, GM_WIDTH), BF16),
            jax.ShapeDtypeStruct((m, RET_WIDTH), BF16),
            jax.ShapeDtypeStruct((m, RET_WIDTH), BF16),
            jax.ShapeDtypeStruct((m, RET_WIDTH), BF16),
            jax.ShapeDtypeStruct((m, RET_WIDTH), BF16),
            jax.ShapeDtypeStruct((m, CONV_WIDTH), F32),
        ],
        compiler_params=pltpu.CompilerParams(
            dimension_semantics=("parallel",), vmem_limit_bytes=VMEM_LIMIT_BYTES),
        name="in_proj",
    )(x2, g1, w_in, cos2, sin2, lng, lnb, ws, bs)


def _ret_state_body(gc_f, gc_b, n_blocks, kf_ref, vf_ref, kb_ref, vb_ref, zf_ref, zb_ref,
                    hmain_ref, hprev_ref, hnext_ref, cw_ref, cb_ref, clg_ref, clb_ref,
                    sf_ref, sb_ref, ycv_ref, st_ref, hpad_ref):
    i = pl.program_id(1)

    @pl.when(i == 0)
    def _():
        st_ref[...] = jnp.zeros_like(st_ref)

    rows_per_step = STATE_CHUNKS * CHUNK
    zero_halo = jnp.zeros((HALO_ROWS, CONV_WIDTH), F32)
    hpad_ref[0:HALO_ROWS, :] = jnp.where(i == 0, zero_halo, hprev_ref[...])
    hpad_ref[HALO_ROWS:HALO_ROWS + rows_per_step, :] = hmain_ref[...]
    hpad_ref[HALO_ROWS + rows_per_step:, :] = jnp.where(i == n_blocks - 1, zero_halo, hnext_ref[...])
    span = CONV_ROW_BLOCK + SUBLANES
    for rb in range(rows_per_step // CONV_ROW_BLOCK):
        r0 = rb * CONV_ROW_BLOCK
        halves = []
        for ch in range(CONV_WIDTH // LANES):
            cols = slice(ch * LANES, (ch + 1) * LANES)
            acc = None
            for r in range(SUBLANES):
                part = None
                for j in range(CONV_KERNEL):
                    if (j - CONV_PAD) % SUBLANES != r:
                        continue
                    start = r0 + HALO_ROWS + (j - CONV_PAD - r)
                    term = cw_ref[j:j + 1, cols] * hpad_ref[start:start + span, cols]
                    part = term if part is None else part + term
                if r:
                    part = pltpu.roll(part, span - r, 0)
                part = part[:CONV_ROW_BLOCK]
                acc = part if acc is None else acc + part
            halves.append(acc)
        acc = jnp.concatenate(halves, axis=1)
        hc = _standardize(acc + cb_ref[...]) * clg_ref[...] + clb_ref[...]
        ycv_ref[r0:r0 + CONV_ROW_BLOCK, :] = (hc * _sigmoid(hc)).astype(BF16)

    def chunk_kv(k_ref, v_ref, z, j):
        rows = slice(j * CHUNK, (j + 1) * CHUNK)
        vz = v_ref[rows, :] * z
        out = []
        for hd in range(RET_HEADS):
            cols = slice(hd * RET_HEAD_DIM, (hd + 1) * RET_HEAD_DIM)
            out.append(lax.dot_general(k_ref[rows, cols], vz[:, cols], (((0,), (0,)), ((), ())),
                                       preferred_element_type=F32))
        return out

    for d, (k_ref, v_ref, z_ref, s_ref, gc, order) in enumerate((
            (kf_ref, vf_ref, zf_ref, sf_ref, gc_f, range(STATE_CHUNKS)),
            (kb_ref, vb_ref, zb_ref, sb_ref, gc_b, range(STATE_CHUNKS - 1, -1, -1)))):
        z = z_ref[...].astype(BF16)
        for j in order:
            kv = chunk_kv(k_ref, v_ref, z, j)
            for hd in range(RET_HEADS):
                state = st_ref[d, hd]
                s_ref[j, hd] = state.astype(BF16)
                st_ref[d, hd] = gc[hd] * state + kv[hd]


def _ret_state(k, v, zf, zb, gc_f, gc_b, hglu, cw, cb, clg, clb, batch, seq):
    rows = STATE_CHUNKS * CHUNK
    nb = seq // rows
    m = batch * seq
    n_chunks = m // CHUNK
    halo_per_step = rows // HALO_ROWS
    n_halo = m // HALO_ROWS
    step = lambda b, i: b * nb + i
    fwd = pl.BlockSpec((rows, RET_WIDTH), lambda b, i: (step(b, i), 0))
    bwd = pl.BlockSpec((rows, RET_WIDTH), lambda b, i: (b * nb + nb - 1 - i, 0))
    const = lambda shape: pl.BlockSpec(shape, lambda b, i: (0,) * len(shape))
    state_shape = (STATE_CHUNKS, RET_HEADS, RET_HEAD_DIM, RET_HEAD_DIM)
    return pl.pallas_call(
        functools.partial(_ret_state_body, gc_f, gc_b, nb),
        grid=(batch, nb),
        in_specs=[
            fwd, fwd, bwd, bwd, const((CHUNK, RET_WIDTH)), const((CHUNK, RET_WIDTH)),
            pl.BlockSpec((rows, CONV_WIDTH), lambda b, i: (step(b, i), 0)),
            pl.BlockSpec((HALO_ROWS, CONV_WIDTH),
                         lambda b, i: (jnp.maximum(step(b, i) * halo_per_step - 1, 0), 0)),
            pl.BlockSpec((HALO_ROWS, CONV_WIDTH),
                         lambda b, i: (jnp.minimum((step(b, i) + 1) * halo_per_step, n_halo - 1), 0)),
            const((CONV_KERNEL, CONV_WIDTH)), const((1, CONV_WIDTH)), const((1, CONV_WIDTH)),
            const((1, CONV_WIDTH)),
        ],
        out_specs=[
            pl.BlockSpec(state_shape, lambda b, i: (step(b, i), 0, 0, 0)),
            pl.BlockSpec(state_shape, lambda b, i: (b * nb + nb - 1 - i, 0, 0, 0)),
            pl.BlockSpec((rows, CONV_WIDTH), lambda b, i: (step(b, i), 0)),
        ],
        out_shape=[jax.ShapeDtypeStruct((n_chunks, RET_HEADS, RET_HEAD_DIM, RET_HEAD_DIM), BF16)] * 2
        + [jax.ShapeDtypeStruct((m, CONV_WIDTH), BF16)],
        scratch_shapes=[pltpu.VMEM((2, RET_HEADS, RET_HEAD_DIM, RET_HEAD_DIM), F32),
                        pltpu.VMEM((rows + 2 * HALO_ROWS, CONV_WIDTH), F32)],
        compiler_params=pltpu.CompilerParams(
            dimension_semantics=("parallel", "arbitrary"), vmem_limit_bytes=VMEM_LIMIT_BYTES),
        name="ret_state",
    )(k, v, k, v, zf, zb, hglu, hglu, hglu, cw, cb, clg, clb)


def _mix_ffn_body(final, x_ref, ygm_ref, q_ref, k_ref, v_ref, sg_ref, ycv_ref, sf_ref, sb_ref,
                  dmask_ref, xif_ref, xib_ref, wo_ref, g2_ref, win_ref, wdn_ref, gf_ref,
                  o_ref, p_ref, oret_ref, yret_ref, x1_ref):
    pairs = [(c, hd) for c in range(TOKEN_TILE // CHUNK) for hd in range(RET_HEADS)]

    def block(c, hd):
        return slice(c * CHUNK, (c + 1) * CHUNK), slice(hd * RET_HEAD_DIM, (hd + 1) * RET_HEAD_DIM)

    def masked_scores(i):
        c, hd = pairs[i]
        rows, cols = block(c, hd)
        scores = lax.dot_general(q_ref[rows, cols], k_ref[rows, cols], (((1,), (1,)), ((), ())),
                                 preferred_element_type=F32)
        p_ref[i] = (scores * dmask_ref[hd]).astype(BF16)

    def weighted_values(i):
        c, hd = pairs[i]
        rows, cols = block(c, hd)
        qh = q_ref[rows, cols]
        o = jnp.dot(p_ref[i], v_ref[rows, cols], preferred_element_type=F32)
        states = jnp.concatenate([sf_ref[c, hd], sb_ref[c, hd]], axis=1)
        cross = jnp.dot(qh, states, preferred_element_type=F32)
        o = o + xif_ref[:, cols] * cross[:, :RET_HEAD_DIM] + xib_ref[:, cols] * cross[:, RET_HEAD_DIM:]
        oret_ref[rows, cols] = o

    def head_norm_gate(hd):
        cols = slice(hd * RET_HEAD_DIM, (hd + 1) * RET_HEAD_DIM)
        yret_ref[:, cols] = (_standardize(oret_ref[:, cols])
                             * sg_ref[:, cols].astype(F32)).astype(BF16)

    for i in range(len(pairs)):
        masked_scores(i)
    for i in range(len(pairs)):
        weighted_values(i)
    for hd in range(RET_HEADS):
        head_norm_gate(hd)
    x1_ref[...] = (x_ref[...]
                   + jnp.dot(ygm_ref[...], wo_ref[0:GM_WIDTH, :], preferred_element_type=F32)
                   + jnp.dot(yret_ref[...], wo_ref[GM_WIDTH:GM_WIDTH + RET_WIDTH, :],
                             preferred_element_type=F32)
                   + jnp.dot(ycv_ref[...], wo_ref[GM_WIDTH + RET_WIDTH:, :],
                             preferred_element_type=F32))

    x1 = x1_ref[...]
    rms = _rms_factor(x1)
    h = (x1 * g2_ref[...]).astype(BF16)
    acc = None
    for lo in range(0, FFN_HIDDEN, FFN_COL_BLOCK):
        hi = min(lo + FFN_COL_BLOCK, FFN_HIDDEN)
        gate = rms * jnp.dot(h, win_ref[:, lo:hi], preferred_element_type=F32)
        up = rms * jnp.dot(h, win_ref[:, FFN_HIDDEN + lo:FFN_HIDDEN + hi],
                           preferred_element_type=F32)
        act = (gate * _sigmoid(gate) * up).astype(BF16)
        part = jnp.dot(act, wdn_ref[lo:hi, :], preferred_element_type=F32)
        acc = part if acc is None else acc + part
    acc = x1_ref[...] + acc
    o_ref[...] = _rmsnorm(acc, gf_ref[...]) if final else acc


def _mix_ffn(x2, ygm, q, k, v, sg, ycv, sf, sb, dmask, xif, xib, wo, g2, win, wdn, layer, gf, final):
    m = x2.shape[0]
    tok = lambda width: pl.BlockSpec((TOKEN_TILE, width), lambda t: (t, 0))
    const = lambda shape: pl.BlockSpec(shape, lambda t: (0,) * len(shape))
    state = pl.BlockSpec((TOKEN_TILE // CHUNK, RET_HEADS, RET_HEAD_DIM, RET_HEAD_DIM),
                         lambda t: (t, 0, 0, 0))
    return pl.pallas_call(
        functools.partial(_mix_ffn_body, final),
        grid=(m // TOKEN_TILE,),
        in_specs=[
            tok(D_MODEL), tok(GM_WIDTH), tok(RET_WIDTH), tok(RET_WIDTH), tok(RET_WIDTH), tok(RET_WIDTH),
            tok(CONV_WIDTH), state, state,
            const((RET_HEADS, CHUNK, CHUNK)), const((CHUNK, RET_WIDTH)), const((CHUNK, RET_WIDTH)),
            _layer_weight(layer, D_MODEL, D_MODEL, 1), const((1, D_MODEL)),
            _layer_weight(layer, D_MODEL, 2 * FFN_HIDDEN, 1), _layer_weight(layer, FFN_HIDDEN, D_MODEL, 1),
            const((1, D_MODEL)),
        ],
        out_specs=tok(D_MODEL),
        out_shape=jax.ShapeDtypeStruct((m, D_MODEL), F32),
        scratch_shapes=[pltpu.VMEM((TOKEN_TILE // CHUNK * RET_HEADS, CHUNK, CHUNK), BF16),
                        pltpu.VMEM((TOKEN_TILE, RET_WIDTH), F32),
                        pltpu.VMEM((TOKEN_TILE, RET_WIDTH), BF16),
                        pltpu.VMEM((TOKEN_TILE, D_MODEL), F32)],
        compiler_params=pltpu.CompilerParams(
            dimension_semantics=("parallel",), vmem_limit_bytes=VMEM_LIMIT_BYTES),
        name="mix_ffn_final" if final else "mix_ffn",
    )(x2, ygm, q, k, v, sg, ycv, sf, sb, dmask, xif, xib, wo, g2, win, wdn, gf)


def kernel(x, norm1_g, w_in, gm_ln_g, gm_ln_b, gm_ws, gm_bs, conv_w, conv_b, conv_ln_g, conv_ln_b,
           w_out, norm2_g, w_ffn_in, w_ffn_out, final_g):
    batch, seq, _ = x.shape
    depth = w_in.shape[0]
    assert all(seq % t == 0 for t in (STATE_CHUNKS * CHUNK, TOKEN_TILE, PROJ_TILE))

    dmask, zf, zb, xif, xib, gc_f, gc_b = _retention_tables()
    cos2, sin2 = _rotary_tables(seq)
    row = lambda a: a.reshape(1, -1)

    w_in_b, w_out_b = w_in.astype(BF16), w_out.astype(BF16)
    w_ffn_in_b, w_ffn_out_b = w_ffn_in.astype(BF16), w_ffn_out.astype(BF16)

    x2 = x.reshape(batch * seq, D_MODEL)
    for l in range(depth):
        ws = gm_ws[l].reshape(GM_HEADS * CHUNK, CHUNK).astype(BF16)
        bs = jnp.repeat(gm_bs[l].T, GM_HEAD_DIM, axis=1)
        ygm, q, k, v, sg, hglu = _in_proj(
            x2, row(norm1_g[l]), w_in_b, l, cos2, sin2,
            row(gm_ln_g[l]), row(gm_ln_b[l]), ws, bs, seq)
        sf, sb, ycv = _ret_state(k, v, zf, zb, gc_f, gc_b, hglu, conv_w[l], row(conv_b[l]),
                                 row(conv_ln_g[l]), row(conv_ln_b[l]), batch, seq)
        x2 = _mix_ffn(x2, ygm, q, k, v, sg, ycv, sf, sb, dmask, xif, xib, w_out_b,
                      row(norm2_g[l]), w_ffn_in_b, w_ffn_out_b, l, row(final_g),
                      final=(l == depth - 1))
    return x2.reshape(batch, seq, D_MODEL)
```

```python
import functools

import numpy as np
import jax
import jax.numpy as jnp
from jax import lax
from jax.experimental import pallas as pl
from jax.experimental.pallas import tpu as pltpu

F32 = jnp.float32
BF16 = jnp.bfloat16

D_MODEL = 1024
GM_WIDTH = 256
GM_HEADS = 4
GM_HEAD_DIM = GM_WIDTH // GM_HEADS
CHUNK = 128
RET_WIDTH = 512
RET_HEADS = 4
RET_HEAD_DIM = RET_WIDTH // RET_HEADS
CONV_WIDTH = 256
CONV_KERNEL = 31
CONV_PAD = CONV_KERNEL // 2
IN_WIDTH = 2 * GM_WIDTH + 4 * RET_WIDTH + 2 * CONV_WIDTH
FFN_HIDDEN = 2816
ROPE_BASE = 10000.0
EPS = 1e-6

COL_GM = 0
COL_Q = 2 * GM_WIDTH
COL_V = COL_Q + 2 * RET_WIDTH
COL_CONV = COL_V + 2 * RET_WIDTH

TOKEN_TILE = 512
PROJ_TILE = 1024
HALO_ROWS = 16
STATE_CHUNKS = 16
CONV_ROW_BLOCK = 256
SUBLANES = 8
LANES = 128
FFN_COL_BLOCK = 512
VMEM_LIMIT_BYTES = 56 * 1024 * 1024


def _gelu(x):
    return 0.5 * x * (1.0 + lax.erf(x * np.float32(1.0 / np.sqrt(2.0))))


def _sigmoid(x):
    return 1.0 / (1.0 + jnp.exp(-x))


def _standardize(x):
    mu = jnp.mean(x, axis=-1, keepdims=True)
    xc = x - mu
    var = jnp.mean(xc * xc, axis=-1, keepdims=True)
    return xc * lax.rsqrt(var + EPS)


def _rms_factor(x):
    return lax.rsqrt(jnp.mean(x * x, axis=-1, keepdims=True) + EPS)


def _rmsnorm(x, g):
    return x * _rms_factor(x) * g


def _retention_tables():
    idx = np.arange(CHUNK, dtype=np.float32)
    gamma_f = (1.0 - np.exp2(-5.0 - np.arange(RET_HEADS, dtype=np.float32))).astype(np.float32)
    gamma_b = gamma_f[::-1]
    lf = np.log(gamma_f)[:, None]
    lb = np.log(gamma_b)[:, None]
    diff = idx[:, None] - idx[None, :]
    dmask = np.where(diff >= 0,
                     np.exp(lf[:, :, None] * np.maximum(diff, 0.0)),
                     np.exp(lb[:, :, None] * np.maximum(-diff, 0.0))).astype(np.float32)
    zeta_f = np.exp(lf * (CHUNK - 1 - idx))
    zeta_b = np.exp(lb * idx)
    xi_f = np.exp(lf * (idx + 1))
    xi_b = np.exp(lb * (CHUNK - idx))

    def per_row(t):
        return np.repeat(t.T.astype(np.float32), RET_HEAD_DIM, axis=1)

    gc_f = [float(np.exp(np.float32(l) * np.float32(CHUNK))) for l in lf[:, 0]]
    gc_b = [float(np.exp(np.float32(l) * np.float32(CHUNK))) for l in lb[:, 0]]
    return dmask, per_row(zeta_f), per_row(zeta_b), per_row(xi_f), per_row(xi_b), gc_f, gc_b


def _rotary_tables(seq):
    half = RET_HEAD_DIM // 2
    inv_freq = ROPE_BASE ** (-np.arange(half, dtype=np.float64) / half)
    ang = np.arange(seq, dtype=np.float64)[:, None] * inv_freq[None, :]
    cos, sin = np.cos(ang), np.sin(ang)
    cos2 = np.concatenate([cos, cos], axis=-1).astype(np.float32)
    sin2 = np.concatenate([-sin, sin], axis=-1).astype(np.float32)
    return cos2, sin2


def _layer_weight(layer, rows, cols, buffers=2):
    return pl.BlockSpec((None, rows, cols), lambda t: (layer, 0, 0),
                        pipeline_mode=pl.Buffered(buffers))


def _in_proj_body(x_ref, g1_ref, w_ref, cos_ref, sin_ref, lng_ref, lnb_ref, ws_ref, bs_ref,
                  ygm_ref, q_ref, k_ref, v_ref, sg_ref, hglu_ref):
    h = _rmsnorm(x_ref[...], g1_ref[...]).astype(BF16)

    def proj(lo, hi):
        return jnp.dot(h, w_ref[:, lo:hi], preferred_element_type=F32)

    uv = proj(COL_GM, COL_Q)
    u = _gelu(uv[:, :GM_WIDTH])
    v = _gelu(uv[:, GM_WIDTH:])
    vln = (_standardize(v) * lng_ref[...] + lnb_ref[...]).astype(BF16)
    lane_head = lax.broadcasted_iota(jnp.int32, (CHUNK, GM_WIDTH), 1) // GM_HEAD_DIM
    for c in range(PROJ_TILE // CHUNK):
        rows = slice(c * CHUNK, (c + 1) * CHUNK)
        full = jnp.dot(ws_ref[...], vln[rows], preferred_element_type=F32)
        mixed = full[0:CHUNK]
        for hd in range(1, GM_HEADS):
            mixed = jnp.where(lane_head == hd, full[hd * CHUNK:(hd + 1) * CHUNK], mixed)
        ygm_ref[rows, :] = (u[rows] * (mixed + bs_ref[...])).astype(BF16)

    cos2 = cos_ref[...]
    sin2 = sin_ref[...]
    qk = proj(COL_Q, COL_V)
    scale = np.float32(RET_HEAD_DIM ** -0.5)
    for hd in range(RET_HEADS):
        cols = slice(hd * RET_HEAD_DIM, (hd + 1) * RET_HEAD_DIM)
        t = qk[:, cols]
        q_ref[:, cols] = (t * cos2 + pltpu.roll(t, RET_HEAD_DIM // 2, 1) * sin2).astype(BF16)
        t = qk[:, RET_WIDTH + hd * RET_HEAD_DIM:RET_WIDTH + (hd + 1) * RET_HEAD_DIM]
        k_ref[:, cols] = ((t * cos2 + pltpu.roll(t, RET_HEAD_DIM // 2, 1) * sin2) * scale).astype(BF16)
    vg = proj(COL_V, COL_CONV)
    v_ref[...] = vg[:, :RET_WIDTH].astype(BF16)
    g = vg[:, RET_WIDTH:]
    sg_ref[...] = (g * _sigmoid(g)).astype(BF16)

    ag = proj(COL_CONV, IN_WIDTH)
    hglu_ref[...] = ag[:, :CONV_WIDTH] * _sigmoid(ag[:, CONV_WIDTH:])


def _in_proj(x2, g1, w_in, layer, cos2, sin2, lng, lnb, ws, bs, seq):
    m = x2.shape[0]
    tiles_per_seq = seq // PROJ_TILE
    tok = lambda width: pl.BlockSpec((PROJ_TILE, width), lambda t: (t, 0))
    const = lambda shape: pl.BlockSpec(shape, lambda t: (0,) * len(shape))
    return pl.pallas_call(
        _in_proj_body,
        grid=(m // PROJ_TILE,),
        in_specs=[
            tok(D_MODEL), const((1, D_MODEL)), _layer_weight(layer, D_MODEL, IN_WIDTH),
            pl.BlockSpec((PROJ_TILE, RET_HEAD_DIM), lambda t: (t % tiles_per_seq, 0)),
            pl.BlockSpec((PROJ_TILE, RET_HEAD_DIM), lambda t: (t % tiles_per_seq, 0)),
            const((1, GM_WIDTH)), const((1, GM_WIDTH)),
            const((GM_HEADS * CHUNK, CHUNK)), const((CHUNK, GM_WIDTH)),
        ],
        out_specs=[tok(GM_WIDTH), tok(RET_WIDTH), tok(RET_WIDTH), tok(RET_WIDTH), tok(RET_WIDTH),
                   tok(CONV_WIDTH)],
        out_shape=[
            jax.ShapeDtypeStruct((m, GM_WIDTH), BF16),
            jax.ShapeDtypeStruct((m, RET_WIDTH), BF16),
            jax.ShapeDtypeStruct((m, RET_WIDTH), BF16),
            jax.ShapeDtypeStruct((m, RET_WIDTH), BF16),
            jax.ShapeDtypeStruct((m, RET_WIDTH), BF16),
            jax.ShapeDtypeStruct((m, CONV_WIDTH), F32),
        ],
        compiler_params=pltpu.CompilerParams(
            dimension_semantics=("parallel",), vmem_limit_bytes=VMEM_LIMIT_BYTES),
        name="in_proj",
    )(x2, g1, w_in, cos2, sin2, lng, lnb, ws, bs)


def _ret_state_body(gc_f, gc_b, n_blocks, kf_ref, vf_ref, kb_ref, vb_ref, zf_ref, zb_ref,
                    hmain_ref, hprev_ref, hnext_ref, cw_ref, cb_ref, clg_ref, clb_ref,
                    sf_ref, sb_ref, ycv_ref, st_ref, hpad_ref):
    i = pl.program_id(1)

    @pl.when(i == 0)
    def _():
        st_ref[...] = jnp.zeros_like(st_ref)

    rows_per_step = STATE_CHUNKS * CHUNK
    zero_halo = jnp.zeros((HALO_ROWS, CONV_WIDTH), F32)
    hpad_ref[0:HALO_ROWS, :] = jnp.where(i == 0, zero_halo, hprev_ref[...])
    hpad_ref[HALO_ROWS:HALO_ROWS + rows_per_step, :] = hmain_ref[...]
    hpad_ref[HALO_ROWS + rows_per_step:, :] = jnp.where(i == n_blocks - 1, zero_halo, hnext_ref[...])
    span = CONV_ROW_BLOCK + SUBLANES
    for rb in range(rows_per_step // CONV_ROW_BLOCK):
        r0 = rb * CONV_ROW_BLOCK
        halves = []
        for ch in range(CONV_WIDTH // LANES):
            cols = slice(ch * LANES, (ch + 1) * LANES)
            acc = None
            for r in range(SUBLANES):
                part = None
                for j in range(CONV_KERNEL):
                    if (j - CONV_PAD) % SUBLANES != r:
                        continue
                    start = r0 + HALO_ROWS + (j - CONV_PAD - r)
                    term = cw_ref[j:j + 1, cols] * hpad_ref[start:start + span, cols]
                    part = term if part is None else part + term
                if r:
                    part = pltpu.roll(part, span - r, 0)
                part = part[:CONV_ROW_BLOCK]
                acc = part if acc is None else acc + part
            halves.append(acc)
        acc = jnp.concatenate(halves, axis=1)
        hc = _standardize(acc + cb_ref[...]) * clg_ref[...] + clb_ref[...]
        ycv_ref[r0:r0 + CONV_ROW_BLOCK, :] = (hc * _sigmoid(hc)).astype(BF16)

    def chunk_kv(k_ref, v_ref, z, j):
        rows = slice(j * CHUNK, (j + 1) * CHUNK)
        vz = v_ref[rows, :] * z
        out = []
        for hd in range(RET_HEADS):
            cols = slice(hd * RET_HEAD_DIM, (hd + 1) * RET_HEAD_DIM)
            out.append(lax.dot_general(k_ref[rows, cols], vz[:, cols], (((0,), (0,)), ((), ())),
                                       preferred_element_type=F32))
        return out

    for d, (k_ref, v_ref, z_ref, s_ref, gc, order) in enumerate((
            (kf_ref, vf_ref, zf_ref, sf_ref, gc_f, range(STATE_CHUNKS)),
            (kb_ref, vb_ref, zb_ref, sb_ref, gc_b, range(STATE_CHUNKS - 1, -1, -1)))):
        z = z_ref[...].astype(BF16)
        for j in order:
            kv = chunk_kv(k_ref, v_ref, z, j)
            for hd in range(RET_HEADS):
                state = st_ref[d, hd]
                s_ref[j, hd] = state.astype(BF16)
                st_ref[d, hd] = gc[hd] * state + kv[hd]


def _ret_state(k, v, zf, zb, gc_f, gc_b, hglu, cw, cb, clg, clb, batch, seq):
    rows = STATE_CHUNKS * CHUNK
    nb = seq // rows
    m = batch * seq
    n_chunks = m // CHUNK
    halo_per_step = rows // HALO_ROWS
    n_halo = m // HALO_ROWS
    step = lambda b, i: b * nb + i
    fwd = pl.BlockSpec((rows, RET_WIDTH), lambda b, i: (step(b, i), 0))
    bwd = pl.BlockSpec((rows, RET_WIDTH), lambda b, i: (b * nb + nb - 1 - i, 0))
    const = lambda shape: pl.BlockSpec(shape, lambda b, i: (0,) * len(shape))
    state_shape = (STATE_CHUNKS, RET_HEADS, RET_HEAD_DIM, RET_HEAD_DIM)
    return pl.pallas_call(
        functools.partial(_ret_state_body, gc_f, gc_b, nb),
        grid=(batch, nb),
        in_specs=[
            fwd, fwd, bwd, bwd, const((CHUNK, RET_WIDTH)), const((CHUNK, RET_WIDTH)),
            pl.BlockSpec((rows, CONV_WIDTH), lambda b, i: (step(b, i), 0)),
            pl.BlockSpec((HALO_ROWS, CONV_WIDTH),
                         lambda b, i: (jnp.maximum(step(b, i) * halo_per_step - 1, 0), 0)),
            pl.BlockSpec((HALO_ROWS, CONV_WIDTH),
                         lambda b, i: (jnp.minimum((step(b, i) + 1) * halo_per_step, n_halo - 1), 0)),
            const((CONV_KERNEL, CONV_WIDTH)), const((1, CONV_WIDTH)), const((1, CONV_WIDTH)),
            const((1, CONV_WIDTH)),
        ],
        out_specs=[
            pl.BlockSpec(state_shape, lambda b, i: (step(b, i), 0, 0, 0)),
            pl.BlockSpec(state_shape, lambda b, i: (b * nb + nb - 1 - i, 0, 0, 0)),
            pl.BlockSpec((rows, CONV_WIDTH), lambda b, i: (step(b, i), 0)),
        ],
        out_shape=[jax.ShapeDtypeStruct((n_chunks, RET_HEADS, RET_HEAD_DIM, RET_HEAD_DIM), BF16)] * 2
        + [jax.ShapeDtypeStruct((m, CONV_WIDTH), BF16)],
        scratch_shapes=[pltpu.VMEM((2, RET_HEADS, RET_HEAD_DIM, RET_HEAD_DIM), F32),
                        pltpu.VMEM((rows + 2 * HALO_ROWS, CONV_WIDTH), F32)],
        compiler_params=pltpu.CompilerParams(
            dimension_semantics=("parallel", "arbitrary"), vmem_limit_bytes=VMEM_LIMIT_BYTES),
        name="ret_state",
    )(k, v, k, v, zf, zb, hglu, hglu, hglu, cw, cb, clg, clb)


def _mix_ffn_body(final, x_ref, ygm_ref, q_ref, k_ref, v_ref, sg_ref, ycv_ref, sf_ref, sb_ref,
                  dmask_ref, xif_ref, xib_ref, wo_ref, g2_ref, win_ref, wdn_ref, gf_ref,
                  o_ref, p_ref, yret_ref, x1_ref):
    pairs = [(c, hd) for c in range(TOKEN_TILE // CHUNK) for hd in range(RET_HEADS)]

    def block(c, hd):
        return slice(c * CHUNK, (c + 1) * CHUNK), slice(hd * RET_HEAD_DIM, (hd + 1) * RET_HEAD_DIM)

    def masked_scores(i):
        c, hd = pairs[i]
        rows, cols = block(c, hd)
        scores = lax.dot_general(q_ref[rows, cols], k_ref[rows, cols], (((1,), (1,)), ((), ())),
                                 preferred_element_type=F32)
        p_ref[i] = (scores * dmask_ref[hd]).astype(BF16)

    def weighted_values(i):
        c, hd = pairs[i]
        rows, cols = block(c, hd)
        qh = q_ref[rows, cols]
        o = jnp.dot(p_ref[i], v_ref[rows, cols], preferred_element_type=F32)
        states = jnp.concatenate([sf_ref[c, hd], sb_ref[c, hd]], axis=1)
        cross = jnp.dot(qh, states, preferred_element_type=F32)
        o = o + xif_ref[:, cols] * cross[:, :RET_HEAD_DIM] + xib_ref[:, cols] * cross[:, RET_HEAD_DIM:]
        yret_ref[rows, cols] = (_standardize(o) * sg_ref[rows, cols].astype(F32)).astype(BF16)

    for i in range(len(pairs)):
        masked_scores(i)
    for i in range(len(pairs)):
        weighted_values(i)
    x1_ref[...] = (x_ref[...]
                   + jnp.dot(ygm_ref[...], wo_ref[0:GM_WIDTH, :], preferred_element_type=F32)
                   + jnp.dot(yret_ref[...], wo_ref[GM_WIDTH:GM_WIDTH + RET_WIDTH, :],
                             preferred_element_type=F32)
                   + jnp.dot(ycv_ref[...], wo_ref[GM_WIDTH + RET_WIDTH:, :],
                             preferred_element_type=F32))

    x1 = x1_ref[...]
    rms = _rms_factor(x1)
    h = (x1 * g2_ref[...]).astype(BF16)
    acc = None
    for lo in range(0, FFN_HIDDEN, FFN_COL_BLOCK):
        hi = min(lo + FFN_COL_BLOCK, FFN_HIDDEN)
        gate = rms * jnp.dot(h, win_ref[:, lo:hi], preferred_element_type=F32)
        up = rms * jnp.dot(h, win_ref[:, FFN_HIDDEN + lo:FFN_HIDDEN + hi],
                           preferred_element_type=F32)
        act = (gate * _sigmoid(gate) * up).astype(BF16)
        part = jnp.dot(act, wdn_ref[lo:hi, :], preferred_element_type=F32)
        acc = part if acc is None else acc + part
    acc = x1_ref[...] + acc
    o_ref[...] = _rmsnorm(acc, gf_ref[...]) if final else acc


def _mix_ffn(x2, ygm, q, k, v, sg, ycv, sf, sb, dmask, xif, xib, wo, g2, win, wdn, layer, gf, final):
    m = x2.shape[0]
    tok = lambda width: pl.BlockSpec((TOKEN_TILE, width), lambda t: (t, 0))
    const = lambda shape: pl.BlockSpec(shape, lambda t: (0,) * len(shape))
    state = pl.BlockSpec((TOKEN_TILE // CHUNK, RET_HEADS, RET_HEAD_DIM, RET_HEAD_DIM),
                         lambda t: (t, 0, 0, 0))
    return pl.pallas_call(
        functools.partial(_mix_ffn_body, final),
        grid=(m // TOKEN_TILE,),
        in_specs=[
            tok(D_MODEL), tok(GM_WIDTH), tok(RET_WIDTH), tok(RET_WIDTH), tok(RET_WIDTH), tok(RET_WIDTH),
            tok(CONV_WIDTH), state, state,
            const((RET_HEADS, CHUNK, CHUNK)), const((CHUNK, RET_WIDTH)), const((CHUNK, RET_WIDTH)),
            _layer_weight(layer, D_MODEL, D_MODEL, 1), const((1, D_MODEL)),
            _layer_weight(layer, D_MODEL, 2 * FFN_HIDDEN, 1), _layer_weight(layer, FFN_HIDDEN, D_MODEL, 1),
            const((1, D_MODEL)),
        ],
        out_specs=tok(D_MODEL),
        out_shape=jax.ShapeDtypeStruct((m, D_MODEL), F32),
        scratch_shapes=[pltpu.VMEM((TOKEN_TILE // CHUNK * RET_HEADS, CHUNK, CHUNK), BF16),
                        pltpu.VMEM((TOKEN_TILE, RET_WIDTH), BF16),
                        pltpu.VMEM((TOKEN_TILE, D_MODEL), F32)],
        compiler_params=pltpu.CompilerParams(
            dimension_semantics=("parallel",), vmem_limit_bytes=VMEM_LIMIT_BYTES),
        name="mix_ffn_final" if final else "mix_ffn",
    )(x2, ygm, q, k, v, sg, ycv, sf, sb, dmask, xif, xib, wo, g2, win, wdn, gf)


def kernel(x, norm1_g, w_in, gm_ln_g, gm_ln_b, gm_ws, gm_bs, conv_w, conv_b, conv_ln_g, conv_ln_b,
           w_out, norm2_g, w_ffn_in, w_ffn_out, final_g):
    batch, seq, _ = x.shape
    depth = w_in.shape[0]
    assert all(seq % t == 0 for t in (STATE_CHUNKS * CHUNK, TOKEN_TILE, PROJ_TILE))

    dmask, zf, zb, xif, xib, gc_f, gc_b = _retention_tables()
    cos2, sin2 = _rotary_tables(seq)
    row = lambda a: a.reshape(1, -1)

    w_in_b, w_out_b = w_in.astype(BF16), w_out.astype(BF16)
    w_ffn_in_b, w_ffn_out_b = w_ffn_in.astype(BF16), w_ffn_out.astype(BF16)

    x2 = x.reshape(batch * seq, D_MODEL)
    for l in range(depth):
        ws = gm_ws[l].reshape(GM_HEADS * CHUNK, CHUNK).astype(BF16)
        bs = jnp.repeat(gm_bs[l].T, GM_HEAD_DIM, axis=1)
        ygm, q, k, v, sg, hglu = _in_proj(
            x2, row(norm1_g[l]), w_in_b, l, cos2, sin2,
            row(gm_ln_g[l]), row(gm_ln_b[l]), ws, bs, seq)
        sf, sb, ycv = _ret_state(k, v, zf, zb, gc_f, gc_b, hglu, conv_w[l], row(conv_b[l]),
                                 row(conv_ln_g[l]), row(conv_ln_b[l]), batch, seq)
        x2 = _mix_ffn(x2, ygm, q, k, v, sg, ycv, sf, sb, dmask, xif, xib, w_out_b,
                      row(norm2_g[l]), w_ffn_in_b, w_ffn_out_b, l, row(final_g),
                      final=(l == depth - 1))
    return x2.reshape(batch, seq, D_MODEL)
```

```python
import functools

import numpy as np
import jax
import jax.numpy as jnp
from jax import lax
from jax.experimental import pallas as pl
from jax.experimental.pallas import tpu as pltpu

F32 = jnp.float32
BF16 = jnp.bfloat16

D_MODEL = 1024
GM_WIDTH = 256
GM_HEADS = 4
GM_HEAD_DIM = GM_WIDTH // GM_HEADS
CHUNK = 128
RET_WIDTH = 512
RET_HEADS = 4
RET_HEAD_DIM = RET_WIDTH // RET_HEADS
CONV_WIDTH = 256
CONV_KERNEL = 31
CONV_PAD = CONV_KERNEL // 2
IN_WIDTH = 2 * GM_WIDTH + 4 * RET_WIDTH + 2 * CONV_WIDTH
FFN_HIDDEN = 2816
ROPE_BASE = 10000.0
EPS = 1e-6

COL_GM = 0
COL_Q = 2 * GM_WIDTH
COL_V = COL_Q + 2 * RET_WIDTH
COL_CONV = COL_V + 2 * RET_WIDTH

TOKEN_TILE = 512
PROJ_TILE = 1024
NORM_SPLIT = 4
HALO_ROWS = 16
STATE_CHUNKS = 16
CONV_ROW_BLOCK = 256
SUBLANES = 8
LANES = 128
FFN_COL_BLOCK = 512
VMEM_LIMIT_BYTES = 56 * 1024 * 1024


def _gelu(x):
    return 0.5 * x * (1.0 + lax.erf(x * np.float32(1.0 / np.sqrt(2.0))))


def _sigmoid(x):
    return 1.0 / (1.0 + jnp.exp(-x))


def _standardize(x):
    mu = jnp.mean(x, axis=-1, keepdims=True)
    xc = x - mu
    var = jnp.mean(xc * xc, axis=-1, keepdims=True)
    return xc * lax.rsqrt(var + EPS)


def _rms_factor(x):
    return lax.rsqrt(jnp.mean(x * x, axis=-1, keepdims=True) + EPS)


def _rmsnorm(x, g):
    return x * _rms_factor(x) * g


def _retention_tables():
    idx = np.arange(CHUNK, dtype=np.float32)
    gamma_f = (1.0 - np.exp2(-5.0 - np.arange(RET_HEADS, dtype=np.float32))).astype(np.float32)
    gamma_b = gamma_f[::-1]
    lf = np.log(gamma_f)[:, None]
    lb = np.log(gamma_b)[:, None]
    diff = idx[:, None] - idx[None, :]
    dmask = np.where(diff >= 0,
                     np.exp(lf[:, :, None] * np.maximum(diff, 0.0)),
                     np.exp(lb[:, :, None] * np.maximum(-diff, 0.0))).astype(np.float32)
    zeta_f = np.exp(lf * (CHUNK - 1 - idx))
    zeta_b = np.exp(lb * idx)
    xi_f = np.exp(lf * (idx + 1))
    xi_b = np.exp(lb * (CHUNK - idx))

    def per_row(t):
        return np.repeat(t.T.astype(np.float32), RET_HEAD_DIM, axis=1)

    gc_f = [float(np.exp(np.float32(l) * np.float32(CHUNK))) for l in lf[:, 0]]
    gc_b = [float(np.exp(np.float32(l) * np.float32(CHUNK))) for l in lb[:, 0]]
    return dmask, per_row(zeta_f), per_row(zeta_b), per_row(xi_f), per_row(xi_b), gc_f, gc_b


def _rotary_tables(seq):
    half = RET_HEAD_DIM // 2
    inv_freq = ROPE_BASE ** (-np.arange(half, dtype=np.float64) / half)
    ang = np.arange(seq, dtype=np.float64)[:, None] * inv_freq[None, :]
    cos, sin = np.cos(ang), np.sin(ang)
    cos2 = np.concatenate([cos, cos], axis=-1).astype(np.float32)
    sin2 = np.concatenate([-sin, sin], axis=-1).astype(np.float32)
    return cos2, sin2


def _layer_weight(layer, rows, cols, buffers=2):
    return pl.BlockSpec((None, rows, cols), lambda t: (layer, 0, 0),
                        pipeline_mode=pl.Buffered(buffers))


def _in_proj_body(x_ref, g1_ref, w_ref, cos_ref, sin_ref, lng_ref, lnb_ref, ws_ref, bs_ref,
                  ygm_ref, q_ref, k_ref, v_ref, sg_ref, hglu_ref):
    block = PROJ_TILE // NORM_SPLIT
    g1 = g1_ref[...]
    h_blocks = [_rmsnorm(x_ref[r0:r0 + block, :], g1).astype(BF16)
                for r0 in range(0, PROJ_TILE, block)]
    h = jnp.concatenate(h_blocks, axis=0)

    def proj(lo, hi):
        return jnp.dot(h, w_ref[:, lo:hi], preferred_element_type=F32)

    uv = jnp.concatenate([jnp.dot(hh, w_ref[:, COL_GM:COL_Q], preferred_element_type=F32)
                          for hh in h_blocks], axis=0)
    u = _gelu(uv[:, :GM_WIDTH])
    v = _gelu(uv[:, GM_WIDTH:])
    vln = (_standardize(v) * lng_ref[...] + lnb_ref[...]).astype(BF16)
    lane_head = lax.broadcasted_iota(jnp.int32, (CHUNK, GM_WIDTH), 1) // GM_HEAD_DIM
    for c in range(PROJ_TILE // CHUNK):
        rows = slice(c * CHUNK, (c + 1) * CHUNK)
        full = jnp.dot(ws_ref[...], vln[rows], preferred_element_type=F32)
        mixed = full[0:CHUNK]
        for hd in range(1, GM_HEADS):
            mixed = jnp.where(lane_head == hd, full[hd * CHUNK:(hd + 1) * CHUNK], mixed)
        ygm_ref[rows, :] = (u[rows] * (mixed + bs_ref[...])).astype(BF16)

    cos2 = cos_ref[...]
    sin2 = sin_ref[...]
    qk = proj(COL_Q, COL_V)
    scale = np.float32(RET_HEAD_DIM ** -0.5)
    for hd in range(RET_HEADS):
        cols = slice(hd * RET_HEAD_DIM, (hd + 1) * RET_HEAD_DIM)
        t = qk[:, cols]
        q_ref[:, cols] = (t * cos2 + pltpu.roll(t, RET_HEAD_DIM // 2, 1) * sin2).astype(BF16)
        t = qk[:, RET_WIDTH + hd * RET_HEAD_DIM:RET_WIDTH + (hd + 1) * RET_HEAD_DIM]
        k_ref[:, cols] = ((t * cos2 + pltpu.roll(t, RET_HEAD_DIM // 2, 1) * sin2) * scale).astype(BF16)
    vg = proj(COL_V, COL_CONV)
    v_ref[...] = vg[:, :RET_WIDTH].astype(BF16)
    g = vg[:, RET_WIDTH:]
    sg_ref[...] = (g * _sigmoid(g)).astype(BF16)

    ag = proj(COL_CONV, IN_WIDTH)
    hglu_ref[...] = ag[:, :CONV_WIDTH] * _sigmoid(ag[:, CONV_WIDTH:])


def _in_proj(x2, g1, w_in, layer, cos2, sin2, lng, lnb, ws, bs, seq):
    m = x2.shape[0]
    tiles_per_seq = seq // PROJ_TILE
    tok = lambda width: pl.BlockSpec((PROJ_TILE, width), lambda t: (t, 0))
    const = lambda shape: pl.BlockSpec(shape, lambda t: (0,) * len(shape))
    return pl.pallas_call(
        _in_proj_body,
        grid=(m // PROJ_TILE,),
        in_specs=[
            tok(D_MODEL), const((1, D_MODEL)), _layer_weight(layer, D_MODEL, IN_WIDTH),
            pl.BlockSpec((PROJ_TILE, RET_HEAD_DIM), lambda t: (t % tiles_per_seq, 0)),
            pl.BlockSpec((PROJ_TILE, RET_HEAD_DIM), lambda t: (t % tiles_per_seq, 0)),
            const((1, GM_WIDTH)), const((1, GM_WIDTH)),
            const((GM_HEADS * CHUNK, CHUNK)), const((CHUNK, GM_WIDTH)),
        ],
        out_specs=[tok(GM_WIDTH), tok(RET_WIDTH), tok(RET_WIDTH), tok(RET_WIDTH), tok(RET_WIDTH),
                   tok(CONV_WIDTH)],
        out_shape=[
            jax.ShapeDtypeStruct((m, GM_WIDTH), BF16),
            jax.ShapeDtypeStruct((m, RET_WIDTH), BF16),
            jax.ShapeDtypeStruct((m, RET_WIDTH), BF16),
            jax.ShapeDtypeStruct((m, RET_WIDTH), BF16),
            jax.ShapeDtypeStruct((m, RET_WIDTH), BF16),
            jax.ShapeDtypeStruct((m, CONV_WIDTH), F32),
        ],
        compiler_params=pltpu.CompilerParams(
            dimension_semantics=("parallel",), vmem_limit_bytes=VMEM_LIMIT_BYTES),
        name="in_proj",
    )(x2, g1, w_in, cos2, sin2, lng, lnb, ws, bs)


def _ret_state_body(gc_f, gc_b, n_blocks, kf_ref, vf_ref, kb_ref, vb_ref, zf_ref, zb_ref,
                    hmain_ref, hprev_ref, hnext_ref, cw_ref, cb_ref, clg_ref, clb_ref,
                    sf_ref, sb_ref, ycv_ref, st_ref, hpad_ref):
    i = pl.program_id(1)

    @pl.when(i == 0)
    def _():
        st_ref[...] = jnp.zeros_like(st_ref)

    rows_per_step = STATE_CHUNKS * CHUNK
    zero_halo = jnp.zeros((HALO_ROWS, CONV_WIDTH), F32)
    hpad_ref[0:HALO_ROWS, :] = jnp.where(i == 0, zero_halo, hprev_ref[...])
    hpad_ref[HALO_ROWS:HALO_ROWS + rows_per_step, :] = hmain_ref[...]
    hpad_ref[HALO_ROWS + rows_per_step:, :] = jnp.where(i == n_blocks - 1, zero_halo, hnext_ref[...])
    span = CONV_ROW_BLOCK + SUBLANES
    for rb in range(rows_per_step // CONV_ROW_BLOCK):
        r0 = rb * CONV_ROW_BLOCK
        halves = []
        for ch in range(CONV_WIDTH // LANES):
            cols = slice(ch * LANES, (ch + 1) * LANES)
            acc = None
            for r in range(SUBLANES):
                part = None
                for j in range(CONV_KERNEL):
                    if (j - CONV_PAD) % SUBLANES != r:
                        continue
                    start = r0 + HALO_ROWS + (j - CONV_PAD - r)
                    term = cw_ref[j:j + 1, cols] * hpad_ref[start:start + span, cols]
                    part = term if part is None else part + term
                if r:
                    part = pltpu.roll(part, span - r, 0)
                part = part[:CONV_ROW_BLOCK]
                acc = part if acc is None else acc + part
            halves.append(acc)
        acc = jnp.concatenate(halves, axis=1)
        hc = _standardize(acc + cb_ref[...]) * clg_ref[...] + clb_ref[...]
        ycv_ref[r0:r0 + CONV_ROW_BLOCK, :] = (hc * _sigmoid(hc)).astype(BF16)

    def chunk_kv(k_ref, v_ref, z, j):
        rows = slice(j * CHUNK, (j + 1) * CHUNK)
        vz = v_ref[rows, :] * z
        out = []
        for hd in range(RET_HEADS):
            cols = slice(hd * RET_HEAD_DIM, (hd + 1) * RET_HEAD_DIM)
            out.append(lax.dot_general(k_ref[rows, cols], vz[:, cols], (((0,), (0,)), ((), ())),
                                       preferred_element_type=F32))
        return out

    for d, (k_ref, v_ref, z_ref, s_ref, gc, order) in enumerate((
            (kf_ref, vf_ref, zf_ref, sf_ref, gc_f, range(STATE_CHUNKS)),
            (kb_ref, vb_ref, zb_ref, sb_ref, gc_b, range(STATE_CHUNKS - 1, -1, -1)))):
        z = z_ref[...].astype(BF16)
        for j in order:
            kv = chunk_kv(k_ref, v_ref, z, j)
            for hd in range(RET_HEADS):
                state = st_ref[d, hd]
                s_ref[j, hd] = state.astype(BF16)
                st_ref[d, hd] = gc[hd] * state + kv[hd]


def _ret_state(k, v, zf, zb, gc_f, gc_b, hglu, cw, cb, clg, clb, batch, seq):
    rows = STATE_CHUNKS * CHUNK
    nb = seq // rows
    m = batch * seq
    n_chunks = m // CHUNK
    halo_per_step = rows // HALO_ROWS
    n_halo = m // HALO_ROWS
    step = lambda b, i: b * nb + i
    fwd = pl.BlockSpec((rows, RET_WIDTH), lambda b, i: (step(b, i), 0))
    bwd = pl.BlockSpec((rows, RET_WIDTH), lambda b, i: (b * nb + nb - 1 - i, 0))
    const = lambda shape: pl.BlockSpec(shape, lambda b, i: (0,) * len(shape))
    state_shape = (STATE_CHUNKS, RET_HEADS, RET_HEAD_DIM, RET_HEAD_DIM)
    return pl.pallas_call(
        functools.partial(_ret_state_body, gc_f, gc_b, nb),
        grid=(batch, nb),
        in_specs=[
            fwd, fwd, bwd, bwd, const((CHUNK, RET_WIDTH)), const((CHUNK, RET_WIDTH)),
            pl.BlockSpec((rows, CONV_WIDTH), lambda b, i: (step(b, i), 0)),
            pl.BlockSpec((HALO_ROWS, CONV_WIDTH),
                         lambda b, i: (jnp.maximum(step(b, i) * halo_per_step - 1, 0), 0)),
            pl.BlockSpec((HALO_ROWS, CONV_WIDTH),
                         lambda b, i: (jnp.minimum((step(b, i) + 1) * halo_per_step, n_halo - 1), 0)),
            const((CONV_KERNEL, CONV_WIDTH)), const((1, CONV_WIDTH)), const((1, CONV_WIDTH)),
            const((1, CONV_WIDTH)),
        ],
        out_specs=[
            pl.BlockSpec(state_shape, lambda b, i: (step(b, i), 0, 0, 0)),
            pl.BlockSpec(state_shape, lambda b, i: (b * nb + nb - 1 - i, 0, 0, 0)),
            pl.BlockSpec((rows, CONV_WIDTH), lambda b, i: (step(b, i), 0)),
        ],
        out_shape=[jax.ShapeDtypeStruct((n_chunks, RET_HEADS, RET_HEAD_DIM, RET_HEAD_DIM), BF16)] * 2
        + [jax.ShapeDtypeStruct((m, CONV_WIDTH), BF16)],
        scratch_shapes=[pltpu.VMEM((2, RET_HEADS, RET_HEAD_DIM, RET_HEAD_DIM), F32),
                        pltpu.VMEM((rows + 2 * HALO_ROWS, CONV_WIDTH), F32)],
        compiler_params=pltpu.CompilerParams(
            dimension_semantics=("parallel", "arbitrary"), vmem_limit_bytes=VMEM_LIMIT_BYTES),
        name="ret_state",
    )(k, v, k, v, zf, zb, hglu, hglu, hglu, cw, cb, clg, clb)


def _mix_ffn_body(final, x_ref, ygm_ref, q_ref, k_ref, v_ref, sg_ref, ycv_ref, sf_ref, sb_ref,
                  dmask_ref, xif_ref, xib_ref, wo_ref, g2_ref, win_ref, wdn_ref, gf_ref,
                  o_ref, p_ref, yret_ref, x1_ref):
    pairs = [(c, hd) for c in range(TOKEN_TILE // CHUNK) for hd in range(RET_HEADS)]

    def block(c, hd):
        return slice(c * CHUNK, (c + 1) * CHUNK), slice(hd * RET_HEAD_DIM, (hd + 1) * RET_HEAD_DIM)

    def masked_scores(i):
        c, hd = pairs[i]
        rows, cols = block(c, hd)
        scores = lax.dot_general(q_ref[rows, cols], k_ref[rows, cols], (((1,), (1,)), ((), ())),
                                 preferred_element_type=F32)
        p_ref[i] = (scores * dmask_ref[hd]).astype(BF16)

    def weighted_values(i):
        c, hd = pairs[i]
        rows, cols = block(c, hd)
        qh = q_ref[rows, cols]
        o = jnp.dot(p_ref[i], v_ref[rows, cols], preferred_element_type=F32)
        states = jnp.concatenate([sf_ref[c, hd], sb_ref[c, hd]], axis=1)
        cross = jnp.dot(qh, states, preferred_element_type=F32)
        o = o + xif_ref[:, cols] * cross[:, :RET_HEAD_DIM] + xib_ref[:, cols] * cross[:, RET_HEAD_DIM:]
        yret_ref[rows, cols] = (_standardize(o) * sg_ref[rows, cols].astype(F32)).astype(BF16)

    for i in range(len(pairs)):
        masked_scores(i)
    for i in range(len(pairs)):
        weighted_values(i)
    x1_ref[...] = (x_ref[...]
                   + jnp.dot(ygm_ref[...], wo_ref[0:GM_WIDTH, :], preferred_element_type=F32)
                   + jnp.dot(yret_ref[...], wo_ref[GM_WIDTH:GM_WIDTH + RET_WIDTH, :],
                             preferred_element_type=F32)
                   + jnp.dot(ycv_ref[...], wo_ref[GM_WIDTH + RET_WIDTH:, :],
                             preferred_element_type=F32))

    x1 = x1_ref[...]
    rms = _rms_factor(x1)
    h = (x1 * g2_ref[...]).astype(BF16)
    acc = None
    for lo in range(0, FFN_HIDDEN, FFN_COL_BLOCK):
        hi = min(lo + FFN_COL_BLOCK, FFN_HIDDEN)
        gate = rms * jnp.dot(h, win_ref[:, lo:hi], preferred_element_type=F32)
        up = rms * jnp.dot(h, win_ref[:, FFN_HIDDEN + lo:FFN_HIDDEN + hi],
                           preferred_element_type=F32)
        act = (gate * _sigmoid(gate) * up).astype(BF16)
        part = jnp.dot(act, wdn_ref[lo:hi, :], preferred_element_type=F32)
        acc = part if acc is None else acc + part
    acc = x1_ref[...] + acc
    o_ref[...] = _rmsnorm(acc, gf_ref[...]) if final else acc


def _mix_ffn(x2, ygm, q, k, v, sg, ycv, sf, sb, dmask, xif, xib, wo, g2, win, wdn, layer, gf, final):
    m = x2.shape[0]
    tok = lambda width: pl.BlockSpec((TOKEN_TILE, width), lambda t: (t, 0))
    const = lambda shape: pl.BlockSpec(shape, lambda t: (0,) * len(shape))
    state = pl.BlockSpec((TOKEN_TILE // CHUNK, RET_HEADS, RET_HEAD_DIM, RET_HEAD_DIM),
                         lambda t: (t, 0, 0, 0))
    return pl.pallas_call(
        functools.partial(_mix_ffn_body, final),
        grid=(m // TOKEN_TILE,),
        in_specs=[
            tok(D_MODEL), tok(GM_WIDTH), tok(RET_WIDTH), tok(RET_WIDTH), tok(RET_WIDTH), tok(RET_WIDTH),
            tok(CONV_WIDTH), state, state,
            const((RET_HEADS, CHUNK, CHUNK)), const((CHUNK, RET_WIDTH)), const((CHUNK, RET_WIDTH)),
            _layer_weight(layer, D_MODEL, D_MODEL, 1), const((1, D_MODEL)),
            _layer_weight(layer, D_MODEL, 2 * FFN_HIDDEN, 1), _layer_weight(layer, FFN_HIDDEN, D_MODEL, 1),
            const((1, D_MODEL)),
        ],
        out_specs=tok(D_MODEL),
        out_shape=jax.ShapeDtypeStruct((m, D_MODEL), F32),
        scratch_shapes=[pltpu.VMEM((TOKEN_TILE // CHUNK * RET_HEADS, CHUNK, CHUNK), BF16),
                        pltpu.VMEM((TOKEN_TILE, RET_WIDTH), BF16),
                        pltpu.VMEM((TOKEN_TILE, D_MODEL), F32)],
        compiler_params=pltpu.CompilerParams(
            dimension_semantics=("parallel",), vmem_limit_bytes=VMEM_LIMIT_BYTES),
        name="mix_ffn_final" if final else "mix_ffn",
    )(x2, ygm, q, k, v, sg, ycv, sf, sb, dmask, xif, xib, wo, g2, win, wdn, gf)


def kernel(x, norm1_g, w_in, gm_ln_g, gm_ln_b, gm_ws, gm_bs, conv_w, conv_b, conv_ln_g, conv_ln_b,
           w_out, norm2_g, w_ffn_in, w_ffn_out, final_g):
    batch, seq, _ = x.shape
    depth = w_in.shape[0]
    assert all(seq % t == 0 for t in (STATE_CHUNKS * CHUNK, TOKEN_TILE, PROJ_TILE))

    dmask, zf, zb, xif, xib, gc_f, gc_b = _retention_tables()
    cos2, sin2 = _rotary_tables(seq)
    row = lambda a: a.reshape(1, -1)

    w_in_b, w_out_b = w_in.astype(BF16), w_out.astype(BF16)
    w_ffn_in_b, w_ffn_out_b = w_ffn_in.astype(BF16), w_ffn_out.astype(BF16)

    x2 = x.reshape(batch * seq, D_MODEL)
    for l in range(depth):
        ws = gm_ws[l].reshape(GM_HEADS * CHUNK, CHUNK).astype(BF16)
        bs = jnp.repeat(gm_bs[l].T, GM_HEAD_DIM, axis=1)
        ygm, q, k, v, sg, hglu = _in_proj(
            x2, row(norm1_g[l]), w_in_b, l, cos2, sin2,
            row(gm_ln_g[l]), row(gm_ln_b[l]), ws, bs, seq)
        sf, sb, ycv = _ret_state(k, v, zf, zb, gc_f, gc_b, hglu, conv_w[l], row(conv_b[l]),
                                 row(conv_ln_g[l]), row(conv_ln_b[l]), batch, seq)
        x2 = _mix_ffn(x2, ygm, q, k, v, sg, ycv, sf, sb, dmask, xif, xib, w_out_b,
                      row(norm2_g[l]), w_ffn_in_b, w_ffn_out_b, l, row(final_g),
                      final=(l == depth - 1))
    return x2.reshape(batch, seq, D_MODEL)
```

```python
import functools

import numpy as np
import jax
import jax.numpy as jnp
from jax import lax
from jax.experimental import pallas as pl
from jax.experimental.pallas import tpu as pltpu

F32 = jnp.float32
BF16 = jnp.bfloat16

D_MODEL = 1024
GM_WIDTH = 256
GM_HEADS = 4
GM_HEAD_DIM = GM_WIDTH // GM_HEADS
CHUNK = 128
RET_WIDTH = 512
RET_HEADS = 4
RET_HEAD_DIM = RET_WIDTH // RET_HEADS
CONV_WIDTH = 256
CONV_KERNEL = 31
CONV_PAD = CONV_KERNEL // 2
IN_WIDTH = 2 * GM_WIDTH + 4 * RET_WIDTH + 2 * CONV_WIDTH
FFN_HIDDEN = 2816
ROPE_BASE = 10000.0
EPS = 1e-6

COL_GM = 0
COL_Q = 2 * GM_WIDTH
COL_V = COL_Q + 2 * RET_WIDTH
COL_CONV = COL_V + 2 * RET_WIDTH

TOKEN_TILE = 512
PROJ_TILE = 1024
NORM_SPLIT = 4
HALO_ROWS = 16
STATE_CHUNKS = 16
CONV_ROW_BLOCK = 256
SUBLANES = 8
LANES = 128
FFN_COL_BLOCK = 512
VMEM_LIMIT_BYTES = 56 * 1024 * 1024


def _gelu(x):
    return 0.5 * x * (1.0 + lax.erf(x * np.float32(1.0 / np.sqrt(2.0))))


def _sigmoid(x):
    return 1.0 / (1.0 + jnp.exp(-x))


def _standardize(x):
    mu = jnp.mean(x, axis=-1, keepdims=True)
    xc = x - mu
    var = jnp.mean(xc * xc, axis=-1, keepdims=True)
    return xc * lax.rsqrt(var + EPS)


def _rms_factor(x):
    return lax.rsqrt(jnp.mean(x * x, axis=-1, keepdims=True) + EPS)


def _rmsnorm(x, g):
    return x * _rms_factor(x) * g


def _retention_tables():
    idx = np.arange(CHUNK, dtype=np.float32)
    gamma_f = (1.0 - np.exp2(-5.0 - np.arange(RET_HEADS, dtype=np.float32))).astype(np.float32)
    gamma_b = gamma_f[::-1]
    lf = np.log(gamma_f)[:, None]
    lb = np.log(gamma_b)[:, None]
    diff = idx[:, None] - idx[None, :]
    dmask = np.where(diff >= 0,
                     np.exp(lf[:, :, None] * np.maximum(diff, 0.0)),
                     np.exp(lb[:, :, None] * np.maximum(-diff, 0.0))).astype(np.float32)
    zeta_f = np.exp(lf * (CHUNK - 1 - idx))
    zeta_b = np.exp(lb * idx)
    xi_f = np.exp(lf * (idx + 1))
    xi_b = np.exp(lb * (CHUNK - idx))

    def per_row(t):
        return np.repeat(t.T.astype(np.float32), RET_HEAD_DIM, axis=1)

    gc_f = [float(np.exp(np.float32(l) * np.float32(CHUNK))) for l in lf[:, 0]]
    gc_b = [float(np.exp(np.float32(l) * np.float32(CHUNK))) for l in lb[:, 0]]
    return dmask, per_row(zeta_f), per_row(zeta_b), per_row(xi_f), per_row(xi_b), gc_f, gc_b


def _rotary_tables(seq):
    half = RET_HEAD_DIM // 2
    inv_freq = ROPE_BASE ** (-np.arange(half, dtype=np.float64) / half)
    ang = np.arange(seq, dtype=np.float64)[:, None] * inv_freq[None, :]
    cos, sin = np.cos(ang), np.sin(ang)
    cos2 = np.concatenate([cos, cos], axis=-1).astype(np.float32)
    sin2 = np.concatenate([-sin, sin], axis=-1).astype(np.float32)
    return cos2, sin2


def _layer_weight(layer, rows, cols, buffers=2):
    return pl.BlockSpec((None, rows, cols), lambda t: (layer, 0, 0),
                        pipeline_mode=pl.Buffered(buffers))


def _in_proj_body(x_ref, g1_ref, w_ref, cos_ref, sin_ref, lng_ref, lnb_ref, ws_ref, bs_ref,
                  ygm_ref, q_ref, k_ref, v_ref, sg_ref, hglu_ref):
    block = PROJ_TILE // NORM_SPLIT
    g1 = g1_ref[...]
    h_blocks = [_rmsnorm(x_ref[r0:r0 + block, :], g1).astype(BF16)
                for r0 in range(0, PROJ_TILE, block)]
    h = jnp.concatenate(h_blocks, axis=0)

    def proj(lo, hi):
        return jnp.dot(h, w_ref[:, lo:hi], preferred_element_type=F32)

    uv = jnp.concatenate([jnp.dot(hh, w_ref[:, COL_GM:COL_Q], preferred_element_type=F32)
                          for hh in h_blocks], axis=0)
    u = _gelu(uv[:, :GM_WIDTH])
    v = _gelu(uv[:, GM_WIDTH:])
    vln = (_standardize(v) * lng_ref[...] + lnb_ref[...]).astype(BF16)
    lane_head = lax.broadcasted_iota(jnp.int32, (CHUNK, GM_WIDTH), 1) // GM_HEAD_DIM
    for c in range(PROJ_TILE // CHUNK):
        rows = slice(c * CHUNK, (c + 1) * CHUNK)
        full = jnp.dot(ws_ref[...], vln[rows], preferred_element_type=F32)
        mixed = full[0:CHUNK]
        for hd in range(1, GM_HEADS):
            mixed = jnp.where(lane_head == hd, full[hd * CHUNK:(hd + 1) * CHUNK], mixed)
        ygm_ref[rows, :] = (u[rows] * (mixed + bs_ref[...])).astype(BF16)

    cos2 = cos_ref[...]
    sin2 = sin_ref[...]
    qk = proj(COL_Q, COL_V)
    scale = np.float32(RET_HEAD_DIM ** -0.5)
    for hd in range(RET_HEADS):
        cols = slice(hd * RET_HEAD_DIM, (hd + 1) * RET_HEAD_DIM)
        t = qk[:, cols]
        q_ref[:, cols] = (t * cos2 + pltpu.roll(t, RET_HEAD_DIM // 2, 1) * sin2).astype(BF16)
        t = qk[:, RET_WIDTH + hd * RET_HEAD_DIM:RET_WIDTH + (hd + 1) * RET_HEAD_DIM]
        k_ref[:, cols] = ((t * cos2 + pltpu.roll(t, RET_HEAD_DIM // 2, 1) * sin2) * scale).astype(BF16)
    vg = proj(COL_V, COL_CONV)
    v_ref[...] = vg[:, :RET_WIDTH].astype(BF16)
    g = vg[:, RET_WIDTH:]
    sg_ref[...] = (g * _sigmoid(g)).astype(BF16)

    ag = proj(COL_CONV, IN_WIDTH)
    hglu_ref[...] = ag[:, :CONV_WIDTH] * _sigmoid(ag[:, CONV_WIDTH:])


def _in_proj(x2, g1, w_in, layer, cos2, sin2, lng, lnb, ws, bs, seq):
    m = x2.shape[0]
    tiles_per_seq = seq // PROJ_TILE
    tok = lambda width: pl.BlockSpec((PROJ_TILE, width), lambda t: (t, 0))
    const = lambda shape: pl.BlockSpec(shape, lambda t: (0,) * len(shape))
    return pl.pallas_call(
        _in_proj_body,
        grid=(m // PROJ_TILE,),
        in_specs=[
            tok(D_MODEL), const((1, D_MODEL)), _layer_weight(layer, D_MODEL, IN_WIDTH),
            pl.BlockSpec((PROJ_TILE, RET_HEAD_DIM), lambda t: (t % tiles_per_seq, 0)),
            pl.BlockSpec((PROJ_TILE, RET_HEAD_DIM), lambda t: (t % tiles_per_seq, 0)),
            const((1, GM_WIDTH)), const((1, GM_WIDTH)),
            const((GM_HEADS * CHUNK, CHUNK)), const((CHUNK, GM_WIDTH)),
        ],
        out_specs=[tok(GM_WIDTH), tok(RET_WIDTH), tok(RET_WIDTH), tok(RET_WIDTH), tok(RET_WIDTH),
                   tok(CONV_WIDTH)],
        out_shape=[
            jax.ShapeDtypeStruct((m, GM_WIDTH), BF16),
            jax.ShapeDtypeStruct((m, RET_WIDTH), BF16),
            jax.ShapeDtypeStruct((m, RET_WIDTH), BF16),
            jax.ShapeDtypeStruct((m, RET_WIDTH), BF16),
            jax.ShapeDtypeStruct((m, RET_WIDTH), BF16),
            jax.ShapeDtypeStruct((m, CONV_WIDTH), F32),
        ],
        compiler_params=pltpu.CompilerParams(
            dimension_semantics=("parallel",), vmem_limit_bytes=VMEM_LIMIT_BYTES),
        name="in_proj",
    )(x2, g1, w_in, cos2, sin2, lng, lnb, ws, bs)


def _ret_state_body(gc_f, gc_b, n_blocks, kf_ref, vf_ref, kb_ref, vb_ref, zf_ref, zb_ref,
                    hmain_ref, hprev_ref, hnext_ref, cw_ref, cb_ref, clg_ref, clb_ref,
                    sf_ref, sb_ref, ycv_ref, st_ref, hpad_ref):
    i = pl.program_id(1)

    @pl.when(i == 0)
    def _():
        st_ref[...] = jnp.zeros_like(st_ref)

    rows_per_step = STATE_CHUNKS * CHUNK
    zero_halo = jnp.zeros((HALO_ROWS, CONV_WIDTH), F32)
    hpad_ref[0:HALO_ROWS, :] = jnp.where(i == 0, zero_halo, hprev_ref[...])
    hpad_ref[HALO_ROWS:HALO_ROWS + rows_per_step, :] = hmain_ref[...]
    hpad_ref[HALO_ROWS + rows_per_step:, :] = jnp.where(i == n_blocks - 1, zero_halo, hnext_ref[...])
    span = CONV_ROW_BLOCK + SUBLANES
    for rb in range(rows_per_step // CONV_ROW_BLOCK):
        r0 = rb * CONV_ROW_BLOCK
        halves = []
        for ch in range(CONV_WIDTH // LANES):
            cols = slice(ch * LANES, (ch + 1) * LANES)
            acc = None
            for r in reversed(range(SUBLANES)):
                part = None
                for j in range(CONV_KERNEL):
                    if (j - CONV_PAD) % SUBLANES != r:
                        continue
                    start = r0 + HALO_ROWS + (j - CONV_PAD - r)
                    term = cw_ref[j:j + 1, cols] * hpad_ref[start:start + span, cols]
                    part = term if part is None else part + term
                acc = part if acc is None else part + pltpu.roll(acc, span - 1, 0)
            halves.append(acc[:CONV_ROW_BLOCK])
        acc = jnp.concatenate(halves, axis=1)
        hc = _standardize(acc + cb_ref[...]) * clg_ref[...] + clb_ref[...]
        ycv_ref[r0:r0 + CONV_ROW_BLOCK, :] = (hc * _sigmoid(hc)).astype(BF16)

    def chunk_kv(k_ref, v_ref, z, j):
        rows = slice(j * CHUNK, (j + 1) * CHUNK)
        vz = v_ref[rows, :] * z
        out = []
        for hd in range(RET_HEADS):
            cols = slice(hd * RET_HEAD_DIM, (hd + 1) * RET_HEAD_DIM)
            out.append(lax.dot_general(k_ref[rows, cols], vz[:, cols], (((0,), (0,)), ((), ())),
                                       preferred_element_type=F32))
        return out

    for d, (k_ref, v_ref, z_ref, s_ref, gc, order) in enumerate((
            (kf_ref, vf_ref, zf_ref, sf_ref, gc_f, range(STATE_CHUNKS)),
            (kb_ref, vb_ref, zb_ref, sb_ref, gc_b, range(STATE_CHUNKS - 1, -1, -1)))):
        z = z_ref[...].astype(BF16)
        for j in order:
            kv = chunk_kv(k_ref, v_ref, z, j)
            for hd in range(RET_HEADS):
                state = st_ref[d, hd]
                s_ref[j, hd] = state.astype(BF16)
                st_ref[d, hd] = gc[hd] * state + kv[hd]


def _ret_state(k, v, zf, zb, gc_f, gc_b, hglu, cw, cb, clg, clb, batch, seq):
    rows = STATE_CHUNKS * CHUNK
    nb = seq // rows
    m = batch * seq
    n_chunks = m // CHUNK
    halo_per_step = rows // HALO_ROWS
    n_halo = m // HALO_ROWS
    step = lambda b, i: b * nb + i
    fwd = pl.BlockSpec((rows, RET_WIDTH), lambda b, i: (step(b, i), 0))
    bwd = pl.BlockSpec((rows, RET_WIDTH), lambda b, i: (b * nb + nb - 1 - i, 0))
    const = lambda shape: pl.BlockSpec(shape, lambda b, i: (0,) * len(shape))
    state_shape = (STATE_CHUNKS, RET_HEADS, RET_HEAD_DIM, RET_HEAD_DIM)
    return pl.pallas_call(
        functools.partial(_ret_state_body, gc_f, gc_b, nb),
        grid=(batch, nb),
        in_specs=[
            fwd, fwd, bwd, bwd, const((CHUNK, RET_WIDTH)), const((CHUNK, RET_WIDTH)),
            pl.BlockSpec((rows, CONV_WIDTH), lambda b, i: (step(b, i), 0)),
            pl.BlockSpec((HALO_ROWS, CONV_WIDTH),
                         lambda b, i: (jnp.maximum(step(b, i) * halo_per_step - 1, 0), 0)),
            pl.BlockSpec((HALO_ROWS, CONV_WIDTH),
                         lambda b, i: (jnp.minimum((step(b, i) + 1) * halo_per_step, n_halo - 1), 0)),
            const((CONV_KERNEL, CONV_WIDTH)), const((1, CONV_WIDTH)), const((1, CONV_WIDTH)),
            const((1, CONV_WIDTH)),
        ],
        out_specs=[
            pl.BlockSpec(state_shape, lambda b, i: (step(b, i), 0, 0, 0)),
            pl.BlockSpec(state_shape, lambda b, i: (b * nb + nb - 1 - i, 0, 0, 0)),
            pl.BlockSpec((rows, CONV_WIDTH), lambda b, i: (step(b, i), 0)),
        ],
        out_shape=[jax.ShapeDtypeStruct((n_chunks, RET_HEADS, RET_HEAD_DIM, RET_HEAD_DIM), BF16)] * 2
        + [jax.ShapeDtypeStruct((m, CONV_WIDTH), BF16)],
        scratch_shapes=[pltpu.VMEM((2, RET_HEADS, RET_HEAD_DIM, RET_HEAD_DIM), F32),
                        pltpu.VMEM((rows + 2 * HALO_ROWS, CONV_WIDTH), F32)],
        compiler_params=pltpu.CompilerParams(
            dimension_semantics=("parallel", "arbitrary"), vmem_limit_bytes=VMEM_LIMIT_BYTES),
        name="ret_state",
    )(k, v, k, v, zf, zb, hglu, hglu, hglu, cw, cb, clg, clb)


def _mix_ffn_body(final, x_ref, ygm_ref, q_ref, k_ref, v_ref, sg_ref, ycv_ref, sf_ref, sb_ref,
                  dmask_ref, xif_ref, xib_ref, wo_ref, g2_ref, win_ref, wdn_ref, gf_ref,
                  o_ref, p_ref, yret_ref, x1_ref):
    pairs = [(c, hd) for c in range(TOKEN_TILE // CHUNK) for hd in range(RET_HEADS)]

    def block(c, hd):
        return slice(c * CHUNK, (c + 1) * CHUNK), slice(hd * RET_HEAD_DIM, (hd + 1) * RET_HEAD_DIM)

    def masked_scores(i):
        c, hd = pairs[i]
        rows, cols = block(c, hd)
        scores = lax.dot_general(q_ref[rows, cols], k_ref[rows, cols], (((1,), (1,)), ((), ())),
                                 preferred_element_type=F32)
        p_ref[i] = (scores * dmask_ref[hd]).astype(BF16)

    def weighted_values(i):
        c, hd = pairs[i]
        rows, cols = block(c, hd)
        qh = q_ref[rows, cols]
        o = jnp.dot(p_ref[i], v_ref[rows, cols], preferred_element_type=F32)
        states = jnp.concatenate([sf_ref[c, hd], sb_ref[c, hd]], axis=1)
        cross = jnp.dot(qh, states, preferred_element_type=F32)
        o = o + xif_ref[:, cols] * cross[:, :RET_HEAD_DIM] + xib_ref[:, cols] * cross[:, RET_HEAD_DIM:]
        yret_ref[rows, cols] = (_standardize(o) * sg_ref[rows, cols].astype(F32)).astype(BF16)

    for i in range(len(pairs)):
        masked_scores(i)
    for i in range(len(pairs)):
        weighted_values(i)
    x1_ref[...] = (x_ref[...]
                   + jnp.dot(ygm_ref[...], wo_ref[0:GM_WIDTH, :], preferred_element_type=F32)
                   + jnp.dot(yret_ref[...], wo_ref[GM_WIDTH:GM_WIDTH + RET_WIDTH, :],
                             preferred_element_type=F32)
                   + jnp.dot(ycv_ref[...], wo_ref[GM_WIDTH + RET_WIDTH:, :],
                             preferred_element_type=F32))

    x1 = x1_ref[...]
    rms = _rms_factor(x1)
    h = (x1 * g2_ref[...]).astype(BF16)
    acc = None
    for lo in range(0, FFN_HIDDEN, FFN_COL_BLOCK):
        hi = min(lo + FFN_COL_BLOCK, FFN_HIDDEN)
        gate = rms * jnp.dot(h, win_ref[:, lo:hi], preferred_element_type=F32)
        up = rms * jnp.dot(h, win_ref[:, FFN_HIDDEN + lo:FFN_HIDDEN + hi],
                           preferred_element_type=F32)
        act = (gate * _sigmoid(gate) * up).astype(BF16)
        part = jnp.dot(act, wdn_ref[lo:hi, :], preferred_element_type=F32)
        acc = part if acc is None else acc + part
    acc = x1_ref[...] + acc
    o_ref[...] = _rmsnorm(acc, gf_ref[...]) if final else acc


def _mix_ffn(x2, ygm, q, k, v, sg, ycv, sf, sb, dmask, xif, xib, wo, g2, win, wdn, layer, gf, final):
    m = x2.shape[0]
    tok = lambda width: pl.BlockSpec((TOKEN_TILE, width), lambda t: (t, 0))
    const = lambda shape: pl.BlockSpec(shape, lambda t: (0,) * len(shape))
    state = pl.BlockSpec((TOKEN_TILE // CHUNK, RET_HEADS, RET_HEAD_DIM, RET_HEAD_DIM),
                         lambda t: (t, 0, 0, 0))
    return pl.pallas_call(
        functools.partial(_mix_ffn_body, final),
        grid=(m // TOKEN_TILE,),
        in_specs=[
            tok(D_MODEL), tok(GM_WIDTH), tok(RET_WIDTH), tok(RET_WIDTH), tok(RET_WIDTH), tok(RET_WIDTH),
            tok(CONV_WIDTH), state, state,
            const((RET_HEADS, CHUNK, CHUNK)), const((CHUNK, RET_WIDTH)), const((CHUNK, RET_WIDTH)),
            _layer_weight(layer, D_MODEL, D_MODEL, 1), const((1, D_MODEL)),
            _layer_weight(layer, D_MODEL, 2 * FFN_HIDDEN, 1), _layer_weight(layer, FFN_HIDDEN, D_MODEL, 1),
            const((1, D_MODEL)),
        ],
        out_specs=tok(D_MODEL),
        out_shape=jax.ShapeDtypeStruct((m, D_MODEL), F32),
        scratch_shapes=[pltpu.VMEM((TOKEN_TILE // CHUNK * RET_HEADS, CHUNK, CHUNK), BF16),
                        pltpu.VMEM((TOKEN_TILE, RET_WIDTH), BF16),
                        pltpu.VMEM((TOKEN_TILE, D_MODEL), F32)],
        compiler_params=pltpu.CompilerParams(
            dimension_semantics=("parallel",), vmem_limit_bytes=VMEM_LIMIT_BYTES),
        name="mix_ffn_final" if final else "mix_ffn",
    )(x2, ygm, q, k, v, sg, ycv, sf, sb, dmask, xif, xib, wo, g2, win, wdn, gf)


def kernel(x, norm1_g, w_in, gm_ln_g, gm_ln_b, gm_ws, gm_bs, conv_w, conv_b, conv_ln_g, conv_ln_b,
           w_out, norm2_g, w_ffn_in, w_ffn_out, final_g):
    batch, seq, _ = x.shape
    depth = w_in.shape[0]
    assert all(seq % t == 0 for t in (STATE_CHUNKS * CHUNK, TOKEN_TILE, PROJ_TILE))

    dmask, zf, zb, xif, xib, gc_f, gc_b = _retention_tables()
    cos2, sin2 = _rotary_tables(seq)
    row = lambda a: a.reshape(1, -1)

    w_in_b, w_out_b = w_in.astype(BF16), w_out.astype(BF16)
    w_ffn_in_b, w_ffn_out_b = w_ffn_in.astype(BF16), w_ffn_out.astype(BF16)

    x2 = x.reshape(batch * seq, D_MODEL)
    for l in range(depth):
        ws = gm_ws[l].reshape(GM_HEADS * CHUNK, CHUNK).astype(BF16)
        bs = jnp.repeat(gm_bs[l].T, GM_HEAD_DIM, axis=1)
        ygm, q, k, v, sg, hglu = _in_proj(
            x2, row(norm1_g[l]), w_in_b, l, cos2, sin2,
            row(gm_ln_g[l]), row(gm_ln_b[l]), ws, bs, seq)
        sf, sb, ycv = _ret_state(k, v, zf, zb, gc_f, gc_b, hglu, conv_w[l], row(conv_b[l]),
                                 row(conv_ln_g[l]), row(conv_ln_b[l]), batch, seq)
        x2 = _mix_ffn(x2, ygm, q, k, v, sg, ycv, sf, sb, dmask, xif, xib, w_out_b,
                      row(norm2_g[l]), w_ffn_in_b, w_ffn_out_b, l, row(final_g),
                      final=(l == depth - 1))
    return x2.reshape(batch, seq, D_MODEL)
```
